```python
import jax, jax.numpy as jnp
from jax import lax
import numpy as np

D_MODEL = 4096
BATCH = 2
SEQ = 4096
DEPTH = 1

MIX_WIDTH = D_MODEL
RWKV_WIDTH = MIX_WIDTH // 2
CONV_WIDTH = MIX_WIDTH - RWKV_WIDTH
RWKV_HEAD_DIM = 64
RWKV_HEADS = RWKV_WIDTH // RWKV_HEAD_DIM
DECAY_LORA = 96
ICLR_LORA = 96
GATE_LORA = 256
RWKV_IN = 3 * RWKV_WIDTH + DECAY_LORA + ICLR_LORA + GATE_LORA
RWKV_SPLITS = (RWKV_WIDTH, 2 * RWKV_WIDTH, 3 * RWKV_WIDTH,
               3 * RWKV_WIDTH + DECAY_LORA, 3 * RWKV_WIDTH + DECAY_LORA + ICLR_LORA)
CONV_IN = 3 * CONV_WIDTH
IN_TOTAL = RWKV_IN + CONV_IN
CONV_K = 3
PEER_HEADS = 8
N_KEYS = 128
N_EXPERTS = N_KEYS * N_KEYS
PEER_QUERY = 256
PEER_HALF = PEER_QUERY // 2
PEER_TOPK = 16
PEER_CHUNK = 128
NORM_EPS = 1e-6
GN_EPS = 64e-5

kernel_name = "hybrid_rwkv7_shortconv_peer"


def rms_norm(x, w):
    xf = x.astype(jnp.float32)
    y = xf * lax.rsqrt(jnp.mean(xf * xf, axis=-1, keepdims=True) + NORM_EPS)
    return (y * w.astype(jnp.float32)).astype(x.dtype)


def shift_right(z, n):
    return jnp.pad(z, ((0, 0), (n, 0), (0, 0)))[:, : z.shape[1]]


def wkv7_scan(r, w, k, v, a, b):
    bsz, _, h, n = r.shape

    def step(state, inp):
        r_t, w_t, k_t, v_t, a_t, b_t = inp
        sa = jnp.einsum('bhij,bhj->bhi', state, a_t)
        state = (state * w_t[:, :, None, :]
                 + sa[..., None] * b_t[:, :, None, :]
                 + v_t[..., None] * k_t[:, :, None, :])
        y_t = jnp.einsum('bhij,bhj->bhi', state, r_t)
        return state, y_t

    xs = tuple(jnp.moveaxis(z, 1, 0) for z in (r, w, k, v, a, b))
    s0 = jnp.zeros((bsz, h, n, n), jnp.float32)
    _, ys = lax.scan(step, s0, xs)
    return jnp.moveaxis(ys, 0, 1)


def rwkv7_time_mix(p, mu, w0, w2, a0, a2, g2, k_k, k_a, r_k, ln_x_w, ln_x_b):
    f32 = jnp.float32
    p = p.astype(f32)
    ps = p + (shift_right(p, 1) - p) * mu.astype(f32)
    r, k, v, wl, al, gl = jnp.split(ps, RWKV_SPLITS, axis=-1)
    w_log = -jax.nn.softplus(-(w0.astype(f32) + jnp.tanh(wl) @ w2.astype(f32))) - 0.5
    decay = jnp.exp(-jnp.exp(w_log))
    a = jax.nn.sigmoid(a0.astype(f32) + al @ a2.astype(f32))
    g = jax.nn.sigmoid(gl) @ g2.astype(f32)
    bsz, s, _ = p.shape
    hv = lambda z: z.reshape(bsz, s, RWKV_HEADS, RWKV_HEAD_DIM)
    hp = lambda z: z.astype(f32).reshape(RWKV_HEADS, RWKV_HEAD_DIM)
    r, k, v, decay, a = hv(r), hv(k), hv(v), hv(decay), hv(a)
    kk = k * hp(k_k)
    kk = kk * lax.rsqrt(jnp.maximum(jnp.sum(kk * kk, axis=-1, keepdims=True), 1e-24))
    k = k * (1.0 + (a - 1.0) * hp(k_a))
    y = wkv7_scan(r, decay, k, v, -kk, kk * a)
    mean = jnp.mean(y, axis=-1, keepdims=True)
    var = jnp.mean(jnp.square(y - mean), axis=-1, keepdims=True)
    y = (y - mean) * lax.rsqrt(var + GN_EPS) * hp(ln_x_w) + hp(ln_x_b)
    y = y + jnp.sum(r * k * hp(r_k), axis=-1, keepdims=True) * v
    return y.reshape(bsz, s, RWKV_WIDTH) * g


def short_conv_mix(p, conv_w):
    b_gate, c_gate, h = jnp.split(p, 3, axis=-1)
    z = c_gate * h
    zc = sum(conv_w[:, j] * shift_right(z, CONV_K - 1 - j) for j in range(CONV_K))
    return b_gate * zc


def peer_ffn(x, w_q, sub_keys, expert_down, expert_up):
    bsz, s, d = x.shape
    q = (x @ w_q).reshape(bsz, s, PEER_HEADS, 2, PEER_HALF)
    scores = jnp.einsum('bshcd,hcnd->bshcn', q, sub_keys).astype(jnp.float32)
    top_s, top_i = lax.top_k(scores, PEER_TOPK)
    cand = top_s[..., 0, :, None] + top_s[..., 1, None, :]
    cand = cand.reshape(bsz, s, PEER_HEADS, PEER_TOPK * PEER_TOPK)
    best_s, best_c = lax.top_k(cand, PEER_TOPK)
    i1 = jnp.take_along_axis(top_i[..., 0, :], best_c // PEER_TOPK, axis=-1)
    i2 = jnp.take_along_axis(top_i[..., 1, :], best_c % PEER_TOPK, axis=-1)
    expert = i1 * N_KEYS + i2
    gate = jax.nn.softmax(best_s, axis=-1).astype(x.dtype)
    n_chunks = (bsz * s) // PEER_CHUNK
    xs = (x.reshape(n_chunks, PEER_CHUNK, d),
          expert.reshape(n_chunks, PEER_CHUNK, PEER_HEADS, PEER_TOPK),
          gate.reshape(n_chunks, PEER_CHUNK, PEER_HEADS, PEER_TOPK))

    def block(args):
        xc, ec, gc = args
        u = jnp.take(expert_down, ec, axis=0)
        act = jax.nn.gelu(jnp.einsum('thkd,td->thk', u, xc), approximate=False)
        vsel = jnp.take(expert_up, ec, axis=0)
        return jnp.einsum('thk,thkd->td', gc * act, vsel)

    out = lax.map(block, xs)
    return out.reshape(bsz, s, d)


def setup_inputs(seed: int = 0) -> dict:
    key = jax.random.key(seed)
    ks = jax.random.split(key, 24)
    f32 = jnp.float32
    nrm = lambda k, shape, scale: jax.random.normal(k, shape, f32) * scale
    L, D, W = DEPTH, D_MODEL, RWKV_WIDTH
    decay_base = jnp.linspace(-6.0, -1.0, W, dtype=f32)
    return {
        "x": nrm(ks[0], (BATCH, SEQ, D), 1.0),
        "norm1_w": 1.0 + nrm(ks[1], (L, D), 0.02),
        "w_in": nrm(ks[2], (L, D, IN_TOTAL), D ** -0.5),
        "mu_shift": jax.random.uniform(ks[3], (L, RWKV_IN), f32),
        "w0": decay_base[None, :] + nrm(ks[4], (L, W), 0.1),
        "w2": nrm(ks[5], (L, DECAY_LORA, W), 0.1 * DECAY_LORA ** -0.5),
        "a0": nrm(ks[6], (L, W), 0.1),
        "a2": nrm(ks[7], (L, ICLR_LORA, W), 0.5 * ICLR_LORA ** -0.5),
        "g2": nrm(ks[8], (L, GATE_LORA, W), GATE_LORA ** -0.5),
        "k_k": 0.85 + nrm(ks[9], (L, W), 0.02),
        "k_a": 1.0 + nrm(ks[10], (L, W), 0.02),
        "r_k": nrm(ks[11], (L, W), 0.1),
        "ln_x_w": 1.0 + nrm(ks[12], (L, W), 0.02),
        "ln_x_b": nrm(ks[13], (L, W), 0.02),
        "conv_w": nrm(ks[14], (L, CONV_WIDTH, CONV_K), CONV_K ** -0.5),
        "w_out": nrm(ks[15], (L, MIX_WIDTH, D), MIX_WIDTH ** -0.5),
        "norm2_w": 1.0 + nrm(ks[16], (L, D), 0.02),
        "w_q": nrm(ks[17], (L, D, PEER_HEADS * PEER_QUERY), D ** -0.5),
        "sub_keys": nrm(ks[18], (L, PEER_HEADS, 2, N_KEYS, PEER_HALF), PEER_HALF ** -0.5),
        "expert_down": nrm(ks[19], (L, N_EXPERTS, D), D ** -0.5),
        "expert_up": nrm(ks[20], (L, N_EXPERTS, D), 0.5),
        "norm_f_w": 1.0 + nrm(ks[21], (D,), 0.02),
    }


def reference(x, norm1_w, w_in, mu_shift, w0, w2, a0, a2, g2, k_k, k_a, r_k, ln_x_w, ln_x_b,
              conv_w, w_out, norm2_w, w_q, sub_keys, expert_down, expert_up, norm_f_w):
    h = x
    for l in range(DEPTH):
        xn = rms_norm(h, norm1_w[l])
        proj = xn @ w_in[l]
        y_rwkv = rwkv7_time_mix(proj[..., :RWKV_IN], mu_shift[l], w0[l], w2[l], a0[l], a2[l], g2[l],
                                k_k[l], k_a[l], r_k[l], ln_x_w[l], ln_x_b[l]).astype(h.dtype)
        y_conv = short_conv_mix(proj[..., RWKV_IN:], conv_w[l])
        mixed = jnp.concatenate([y_rwkv, y_conv], axis=-1)
        h = h + mixed @ w_out[l]
        h = h + peer_ffn(rms_norm(h, norm2_w[l]), w_q[l], sub_keys[l], expert_down[l], expert_up[l])
    return rms_norm(h, norm_f_w)
```

```python
import functools

import jax
import jax.numpy as jnp
from jax import lax
from jax.experimental import pallas as pl
from jax.experimental.pallas import tpu as pltpu

F32 = jnp.float32
BF16 = jnp.bfloat16

HEAD_DIM = 64
LANES = 128
NORM_EPS = 1e-6
GN_EPS = 64e-5
DECAY_LORA = 96
ICLR_LORA = 96
GATE_LORA = 256
LORA_PAD = 512
CONV_K = 3
PEER_HEADS = 8
N_KEYS = 128
PEER_TOPK = 16
WKV_CHUNK = 128
VMEM_LIMIT = 56 * 1024 * 1024

NT_DIMS = (((1,), (1,)), ((), ()))
TN_DIMS = (((0,), (0,)), ((), ()))


def _params(*sem):
    return pltpu.CompilerParams(dimension_semantics=sem, vmem_limit_bytes=VMEM_LIMIT)


def _bdot(a, b):
    return jnp.dot(a.astype(BF16), b.astype(BF16), preferred_element_type=F32)


def _split3(x):
    hi = x.astype(BF16)
    r1 = x - hi.astype(F32)
    mid = r1.astype(BF16)
    lo = (r1 - mid.astype(F32)).astype(BF16)
    return hi, mid, lo


def _dot_exact_lhs(a_bf16, x):
    hi, mid, lo = _split3(x)
    d = lambda t: jnp.dot(a_bf16, t, preferred_element_type=F32)
    return d(hi) + d(mid) + d(lo)


def _dot_exact_rhs(x, b_bf16):
    hi, mid, lo = _split3(x)
    d = lambda t: jnp.dot(t, b_bf16, preferred_element_type=F32)
    return d(hi) + d(mid) + d(lo)


def _head_ones():
    r = lax.broadcasted_iota(jnp.int32, (LANES, LANES), 0) // HEAD_DIM
    c = lax.broadcasted_iota(jnp.int32, (LANES, LANES), 1) // HEAD_DIM
    return (r == c).astype(BF16)


def _head_sum(x):
    ones = _head_ones()
    parts = [_dot_exact_rhs(x[:, j:j + LANES], ones) for j in range(0, x.shape[1], LANES)]
    return jnp.concatenate(parts, axis=1)


def _norm_matmul_kernel(x_ref, nw_ref, w_ref, o_ref, *rest, emit_xn):
    xn_scr = rest[-1]

    @pl.when(pl.program_id(1) == 0)
    def _():
        x = x_ref[...]
        y = x * lax.rsqrt(jnp.mean(x * x, axis=-1, keepdims=True) + NORM_EPS)
        xn_scr[...] = (y * nw_ref[...]).astype(BF16)

    o_ref[...] = jnp.dot(xn_scr[...], w_ref[...], preferred_element_type=F32)
    if emit_xn:
        @pl.when(pl.program_id(1) == 0)
        def _():
            rest[0][...] = xn_scr[...]


def _norm_matmul(x, nw, w_bf16, *, emit_xn, tm=512, tn=512):
    m, d = x.shape
    n = w_bf16.shape[1]
    out_shape = [jax.ShapeDtypeStruct((m, n), F32)]
    out_specs = [pl.BlockSpec((tm, tn), lambda i, j: (i, j))]
    if emit_xn:
        out_shape.append(jax.ShapeDtypeStruct((m, d), BF16))
        out_specs.append(pl.BlockSpec((tm, d), lambda i, j: (i, 0)))
    res = pl.pallas_call(
        functools.partial(_norm_matmul_kernel, emit_xn=emit_xn),
        grid=(m // tm, n // tn),
        in_specs=[pl.BlockSpec((tm, d), lambda i, j: (i, 0)),
                  pl.BlockSpec((1, d), lambda i, j: (0, 0)),
                  pl.BlockSpec((d, tn), lambda i, j: (0, j))],
        out_specs=out_specs,
        out_shape=out_shape,
        scratch_shapes=[pltpu.VMEM((tm, d), BF16)],
        compiler_params=_params("parallel", "arbitrary"),
        name="norm_matmul",
    )(x, nw.reshape(1, d), w_bf16)
    return res if emit_xn else res[0]


def _shift_rows(x, prev8, n, first):
    rows = lax.broadcasted_iota(jnp.int32, x.shape, 0)
    out = pltpu.roll(x, n, 0)
    for j in range(n):
        fill = jnp.where(first, 0.0, prev8[8 - n + j:8 - n + j + 1, :])
        out = jnp.where(rows == j, fill, out)
    return out


def _rwkv_prep_kernel(p_ref, pl_ref, pp_ref, ppl_ref, mu_ref, mul_ref, w0_ref, a0_ref,
                      kk_ref, ka_ref, rk_ref, w2_ref, a2_ref, g2_ref,
                      r_o, k_o, v_o, kk_o, b_o, lw_o, g_o, bonus_o, *, tiles_per_seq):
    first = (pl.program_id(0) % tiles_per_seq) == 0
    w = r_o.shape[1]
    p = p_ref[...]
    ps = p + (_shift_rows(p, pp_ref[...], 1, first) - p) * mu_ref[...]
    q = pl_ref[...]
    qs = q + (_shift_rows(q, ppl_ref[...], 1, first) - q) * mul_ref[...]
    r = ps[:, :w]
    k = ps[:, w:2 * w]
    v = ps[:, 2 * w:]
    w_log = -jax.nn.softplus(-(w0_ref[...] + _bdot(jnp.tanh(qs), w2_ref[...]))) - 0.5
    a = jax.nn.sigmoid(a0_ref[...] + _bdot(qs, a2_ref[...]))
    g = _bdot(jax.nn.sigmoid(qs), g2_ref[...])
    kk = k * kk_ref[...]
    kk = kk * lax.rsqrt(jnp.maximum(_head_sum(kk * kk), 1e-24))
    k2 = k * (1.0 + (a - 1.0) * ka_ref[...])
    r_o[...] = r
    k_o[...] = k2
    v_o[...] = v
    kk_o[...] = kk
    b_o[...] = kk * a
    lw_o[...] = -jnp.exp(w_log)
    g_o[...] = g
    bonus_o[...] = _head_sum(r * k2 * rk_ref[...]) * v


def _rwkv_prep(proj, seq, w, mu_rkv, mu_lora, w0, a0, k_k, k_a, r_k, w2p, a2p, g2p, *, tt=128):
    m = proj.shape[0]
    lora_blk = (6 * w) // LORA_PAD
    row = lambda i: (i, 0)
    prev = lambda i: (jnp.maximum(i * (tt // 8) - 1, 0), 0)
    prev_l = lambda i: (jnp.maximum(i * (tt // 8) - 1, 0), lora_blk)
    const = lambda i: (0, 0)
    vec = pl.BlockSpec((1, w), const)
    out = jax.ShapeDtypeStruct((m, w), F32)
    return pl.pallas_call(
        functools.partial(_rwkv_prep_kernel, tiles_per_seq=seq // tt),
        grid=(m // tt,),
        in_specs=[pl.BlockSpec((tt, 3 * w), row),
                  pl.BlockSpec((tt, LORA_PAD), lambda i: (i, lora_blk)),
                  pl.BlockSpec((8, 3 * w), prev),
                  pl.BlockSpec((8, LORA_PAD), prev_l),
                  pl.BlockSpec((1, 3 * w), const),
                  pl.BlockSpec((1, LORA_PAD), const),
                  vec, vec, vec, vec, vec,
                  pl.BlockSpec((LORA_PAD, w), const),
                  pl.BlockSpec((LORA_PAD, w), const),
                  pl.BlockSpec((LORA_PAD, w), const)],
        out_specs=[pl.BlockSpec((tt, w), row)] * 8,
        out_shape=[out] * 8,
        compiler_params=_params("parallel"),
        name="rwkv_prep",
    )(proj, proj, proj, proj, mu_rkv, mu_lora, w0, a0, k_k, k_a, r_k, w2p, a2p, g2p)


def _wkv_kernel(r_ref, k_ref, v_ref, kk_ref, b_ref, lw_ref, y_ref, s_ref):
    c = WKV_CHUNK

    @pl.when(pl.program_id(2) == 0)
    def _():
        s_ref[...] = jnp.zeros_like(s_ref)

    r = r_ref[...]
    k = k_ref[...]
    v = v_ref[...]
    kk = kk_ref[...]
    b = b_ref[...]
    lw = lw_ref[...]
    row = lax.broadcasted_iota(jnp.int32, (c, c), 0)
    col = lax.broadcasted_iota(jnp.int32, (c, c), 1)
    lower = row >= col
    strict = row > col
    eye = (row == col).astype(F32)
    cum = _dot_exact_lhs(lower.astype(BF16), lw)
    cend = cum[c - 1:c, :]
    cc = cum - cum[c // 2 - 1:c // 2, :]
    e_neg = jnp.exp(-cc)
    e_end = jnp.exp(cend - cum)
    at = -kk * jnp.exp(cc - lw)
    rt = r * jnp.exp(cc)
    bt = b * e_neg
    kt = k * e_neg
    head_a = lax.broadcasted_iota(jnp.int32, (c, LANES), 1) < HEAD_DIM
    lhs = jnp.concatenate([jnp.where(head_a, at, 0.0), jnp.where(head_a, 0.0, at),
                           jnp.where(head_a, rt, 0.0), jnp.where(head_a, 0.0, rt)], axis=0)
    rhs = jnp.concatenate([bt, kt], axis=0)
    gram = lax.dot_general(lhs.astype(BF16), rhs.astype(BF16), NT_DIMS,
                           preferred_element_type=F32)
    s0 = s_ref[...].astype(BF16)
    v16 = v.astype(BF16)
    at0 = -kk * jnp.exp(cum - lw)
    rt0 = r * jnp.exp(cum)
    state_u = lax.dot_general(at0.astype(BF16), s0, NT_DIMS, preferred_element_type=F32)
    state_y = lax.dot_general(rt0.astype(BF16), s0, NT_DIMS, preferred_element_type=F32)
    us = []
    for h in range(2):
        n = jnp.where(strict, gram[h * c:(h + 1) * c, :c], 0.0)
        inv = eye + n
        pw = n
        for _ in range(6):
            pw16 = pw.astype(BF16)
            pw = jnp.dot(pw16, pw16, preferred_element_type=F32)
            inv = inv + _bdot(inv, pw)
        a_ak = jnp.where(strict, gram[h * c:(h + 1) * c, c:], 0.0)
        rhs_u = state_u + jnp.dot(a_ak.astype(BF16), v16, preferred_element_type=F32)
        us.append(_bdot(inv, rhs_u))
    u = jnp.where(head_a, us[0], us[1])
    u16 = u.astype(BF16)
    ys = []
    for h in range(2):
        a_rb = jnp.where(lower, gram[(2 + h) * c:(3 + h) * c, :c], 0.0)
        a_rk = jnp.where(lower, gram[(2 + h) * c:(3 + h) * c, c:], 0.0)
        ys.append(jnp.dot(a_rb.astype(BF16), u16, preferred_element_type=F32)
                  + jnp.dot(a_rk.astype(BF16), v16, preferred_element_type=F32))
    y_ref[...] = state_y + jnp.where(head_a, ys[0], ys[1])
    upd = lax.dot_general(jnp.concatenate([u16, v16], axis=0),
                          jnp.concatenate([b * e_end, k * e_end], axis=0).astype(BF16),
                          TN_DIMS, preferred_element_type=F32)
    srow = lax.broadcasted_iota(jnp.int32, (LANES, LANES), 0) // HEAD_DIM
    scol = lax.broadcasted_iota(jnp.int32, (LANES, LANES), 1) // HEAD_DIM
    s_ref[...] = s_ref[...] * jnp.exp(cend) + jnp.where(srow == scol, upd, 0.0)


def _wkv_scan(r, k, v, kk, b, lw, batch, seq):
    m, w = r.shape
    c = WKV_CHUNK
    nchunk = seq // c
    spec = pl.BlockSpec((c, LANES), lambda bi, p, ci: (bi * nchunk + ci, p))
    return pl.pallas_call(
        _wkv_kernel,
        grid=(batch, w // LANES, nchunk),
        in_specs=[spec] * 6,
        out_specs=spec,
        out_shape=jax.ShapeDtypeStruct((m, w), F32),
        scratch_shapes=[pltpu.VMEM((LANES, LANES), F32)],
        compiler_params=_params("parallel", "parallel", "arbitrary"),
        name="wkv_scan",
    )(r, k, v, kk, b, lw)


def _mix_post_kernel(y_ref, g_ref, bonus_ref, lnw_ref, lnb_ref, bg_ref, cg_ref, hv_ref,
                     cgp_ref, hvp_ref, cw_ref, o_ref, *, tiles_per_seq):
    first = (pl.program_id(0) % tiles_per_seq) == 0
    w = y_ref.shape[1]
    y = y_ref[...]
    mean = _head_sum(y) * (1.0 / HEAD_DIM)
    d = y - mean
    var = _head_sum(d * d) * (1.0 / HEAD_DIM)
    yn = d * lax.rsqrt(var + GN_EPS) * lnw_ref[...] + lnb_ref[...]
    o_ref[:, :w] = ((yn + bonus_ref[...]) * g_ref[...]).astype(BF16)
    z = cg_ref[...] * hv_ref[...]
    zp = cgp_ref[...] * hvp_ref[...]
    cw = cw_ref[...]
    zc = (cw[0:1, :] * _shift_rows(z, zp, 2, first) + cw[1:2, :] * _shift_rows(z, zp, 1, first)
          + cw[2:3, :] * z)
    o_ref[:, w:] = (bg_ref[...] * zc).astype(BF16)


def _mix_post(y, g, bonus, ln_w, ln_b, proj, conv_w3, seq, *, tt=128):
    m, w = y.shape
    row = lambda i: (i, 0)
    const = lambda i: (0, 0)
    prev = lambda j: (lambda i: (jnp.maximum(i * (tt // 8) - 1, 0), j))
    return pl.pallas_call(
        functools.partial(_mix_post_kernel, tiles_per_seq=seq // tt),
        grid=(m // tt,),
        in_specs=[pl.BlockSpec((tt, w), row)] * 3
        + [pl.BlockSpec((1, w), const)] * 2
        + [pl.BlockSpec((tt, w), lambda i, j=j: (i, j)) for j in (3, 4, 5)]
        + [pl.BlockSpec((8, w), prev(4)), pl.BlockSpec((8, w), prev(5)),
           pl.BlockSpec((8, w), const)],
        out_specs=pl.BlockSpec((tt, 2 * w), row),
        out_shape=jax.ShapeDtypeStruct((m, 2 * w), BF16),
        compiler_params=_params("parallel"),
        name="mix_post",
    )(y, g, bonus, ln_w, ln_b, proj, proj, proj, proj, proj, conv_w3)


def _matmul_residual_kernel(a_ref, w_ref, res_ref, o_ref):
    o_ref[...] = res_ref[...] + jnp.dot(a_ref[...], w_ref[...], preferred_element_type=F32)


def _matmul_residual(a_bf16, w_bf16, res, *, tm=1024, tn=512):
    m, kd = a_bf16.shape
    n = w_bf16.shape[1]
    return pl.pallas_call(
        _matmul_residual_kernel,
        grid=(m // tm, n // tn),
        in_specs=[pl.BlockSpec((tm, kd), lambda i, j: (i, 0)),
                  pl.BlockSpec((kd, tn), lambda i, j: (0, j)),
                  pl.BlockSpec((tm, tn), lambda i, j: (i, j))],
        out_specs=pl.BlockSpec((tm, tn), lambda i, j: (i, j)),
        out_shape=jax.ShapeDtypeStruct((m, n), F32),
        compiler_params=_params("parallel", "arbitrary"),
        name="matmul_residual",
    )(a_bf16, w_bf16, res)


def _top_values(s, count):
    tops = []
    for _ in range(count):
        mx = jnp.max(s, axis=0, keepdims=True)
        tops.append(mx)
        s = jnp.where(s >= mx, -jnp.inf, s)
    return tops


def _peer_select_kernel(q_ref, keys_ref, s1_o, s2_o, e1_o, e2_o, tau_o):
    q = q_ref[...]
    half = q.shape[1] // 2
    sc = []
    for c in range(2):
        qc = q[:, c * half:(c + 1) * half]
        sc.append(lax.dot_general(keys_ref[0, c].astype(BF16), qc.astype(BF16), NT_DIMS,
                                  preferred_element_type=F32))
    top1 = _top_values(sc[0], PEER_TOPK)
    top2 = _top_values(sc[1], PEER_TOPK)
    t2 = jnp.concatenate(top2, axis=0)
    cand = jnp.concatenate([a + t2 for a in top1], axis=0)
    best = _top_values(cand, PEER_TOPK)
    cmax = best[0]
    z = jnp.exp(best[0] - cmax)
    for bv in best[1:]:
        z = z + jnp.exp(bv - cmax)
    s1_o[0] = sc[0]
    s2_o[0] = sc[1]
    e1_o[0] = jnp.exp(sc[0] - top1[0]) / z
    e2_o[0] = jnp.exp(sc[1] - top2[0])
    tau_o[0] = jnp.broadcast_to(best[-1], tau_o.shape[1:])


def _peer_select(q, sub_keys, *, tt=512):
    m = q.shape[0]
    heads, _, nk, half = sub_keys.shape
    big = jax.ShapeDtypeStruct((heads, nk, m), F32)
    bspec = pl.BlockSpec((1, nk, tt), lambda i, h: (h, 0, i))
    return pl.pallas_call(
        _peer_select_kernel,
        grid=(m // tt, heads),
        in_specs=[pl.BlockSpec((tt, 2 * half), lambda i, h: (i, h)),
                  pl.BlockSpec((1, 2, nk, half), lambda i, h: (h, 0, 0, 0))],
        out_specs=[bspec] * 4 + [pl.BlockSpec((1, 8, tt), lambda i, h: (h, 0, i))],
        out_shape=[big] * 4 + [jax.ShapeDtypeStruct((heads, 8, m), F32)],
        compiler_params=_params("parallel", "arbitrary"),
        name="peer_select",
    )(q, sub_keys)


def _peer_weights_kernel(s1_ref, s2_ref, e1_ref, e2_ref, tau_ref, o_ref, *, rows_per_step):
    heads = s1_ref.shape[0]
    base = pl.program_id(1) * rows_per_step
    for ii in range(rows_per_step):
        acc = None
        for h in range(heads):
            s1 = s1_ref[h, pl.ds(base + ii, 1), :]
            e1 = e1_ref[h, pl.ds(base + ii, 1), :]
            tau = tau_ref[h, 0:1, :]
            term = jnp.where(s1 + s2_ref[h] >= tau, e1 * e2_ref[h], 0.0)
            acc = term if acc is None else acc + term
        o_ref[:, ii * N_KEYS:(ii + 1) * N_KEYS] = acc.T.astype(BF16)


def _peer_weights(s1, s2, e1, e2, tau, *, tt=512, rows_per_step=4):
    heads, nk, m = s1.shape
    big = pl.BlockSpec((heads, nk, tt), lambda i, e: (0, 0, i))
    return pl.pallas_call(
        functools.partial(_peer_weights_kernel, rows_per_step=rows_per_step),
        grid=(m // tt, nk // rows_per_step),
        in_specs=[big] * 4 + [pl.BlockSpec((heads, 8, tt), lambda i, e: (0, 0, i))],
        out_specs=pl.BlockSpec((tt, rows_per_step * nk), lambda i, e: (i, e)),
        out_shape=jax.ShapeDtypeStruct((m, nk * nk), BF16),
        compiler_params=_params("parallel", "arbitrary"),
        name="peer_weights",
    )(s1, s2, e1, e2, tau)


def _peer_mlp_kernel(x_ref, down_ref, up_ref, w_ref, o_ref):
    @pl.when(pl.program_id(1) == 0)
    def _():
        o_ref[...] = jnp.zeros_like(o_ref)

    act = lax.dot_general(x_ref[...], down_ref[...], NT_DIMS, preferred_element_type=F32)
    gelu = 0.5 * act * (1.0 + lax.erf(act * (2.0 ** -0.5)))
    hid = gelu * w_ref[...].astype(F32)
    o_ref[...] += jnp.dot(hid.astype(BF16), up_ref[...], preferred_element_type=F32)


def _peer_mlp(xn_bf16, down_bf16, up_bf16, wgt_bf16, *, tm=512, te=512):
    m, d = xn_bf16.shape
    ne = down_bf16.shape[0]
    return pl.pallas_call(
        _peer_mlp_kernel,
        grid=(m // tm, ne // te),
        in_specs=[pl.BlockSpec((tm, d), lambda i, e: (i, 0)),
                  pl.BlockSpec((te, d), lambda i, e: (e, 0)),
                  pl.BlockSpec((te, d), lambda i, e: (e, 0)),
                  pl.BlockSpec((tm, te), lambda i, e: (i, e))],
        out_specs=pl.BlockSpec((tm, d), lambda i, e: (i, 0)),
        out_shape=jax.ShapeDtypeStruct((m, d), F32),
        compiler_params=_params("parallel", "arbitrary"),
        name="peer_mlp",
    )(xn_bf16, down_bf16, up_bf16, wgt_bf16)


def _final_norm_kernel(h_ref, p_ref, nw_ref, o_ref):
    x = h_ref[...] + p_ref[...]
    y = x * lax.rsqrt(jnp.mean(x * x, axis=-1, keepdims=True) + NORM_EPS)
    o_ref[...] = y * nw_ref[...]


def _final_norm(h, p, nw, *, tt=256):
    m, d = h.shape
    row = lambda i: (i, 0)
    return pl.pallas_call(
        _final_norm_kernel,
        grid=(m // tt,),
        in_specs=[pl.BlockSpec((tt, d), row), pl.BlockSpec((tt, d), row),
                  pl.BlockSpec((1, d), lambda i: (0, 0))],
        out_specs=pl.BlockSpec((tt, d), row),
        out_shape=jax.ShapeDtypeStruct((m, d), F32),
        compiler_params=_params("parallel"),
        name="final_norm",
    )(h, p, nw.reshape(1, d))


def _pad_rows(mat, start, total):
    return jnp.pad(mat, ((start, total - start - mat.shape[0]), (0, 0)))


def kernel(x, norm1_w, w_in, mu_shift, w0, w2, a0, a2, g2, k_k, k_a, r_k, ln_x_w, ln_x_b,
           conv_w, w_out, norm2_w, w_q, sub_keys, expert_down, expert_up, norm_f_w):
    batch, seq, d = x.shape
    depth = w_in.shape[0]
    w = w0.shape[1]
    rwkv_in = 3 * w + DECAY_LORA + ICLR_LORA + GATE_LORA
    lora_w = rwkv_in - 3 * w
    m = batch * seq
    h = x.reshape(m, d)
    row = lambda t: t.reshape(1, -1)
    for l in range(depth):
        w_cat = jnp.concatenate(
            [w_in[l][:, :3 * w], w_in[l][:, rwkv_in:],
             jnp.pad(w_in[l][:, 3 * w:rwkv_in], ((0, 0), (0, LORA_PAD - lora_w)))],
            axis=1).astype(BF16)
        proj = _norm_matmul(h, norm1_w[l], w_cat, emit_xn=False)
        mu = mu_shift[l]
        mu_lora = jnp.pad(mu[3 * w:], (0, LORA_PAD - lora_w))
        w2p = _pad_rows(w2[l], 0, LORA_PAD).astype(BF16)
        a2p = _pad_rows(a2[l], DECAY_LORA, LORA_PAD).astype(BF16)
        g2p = _pad_rows(g2[l], DECAY_LORA + ICLR_LORA, LORA_PAD).astype(BF16)
        r, k2, v, kk, bvec, lw, g, bonus = _rwkv_prep(
            proj, seq, w, row(mu[:3 * w]), row(mu_lora), row(w0[l]), row(a0[l]), row(k_k[l]),
            row(k_a[l]), row(r_k[l]), w2p, a2p, g2p)
        y = _wkv_scan(r, k2, v, kk, bvec, lw, batch, seq)
        conv_w3 = jnp.pad(conv_w[l].T, ((0, 8 - CONV_K), (0, 0)))
        mixed = _mix_post(y, g, bonus, row(ln_x_w[l]), row(ln_x_b[l]), proj, conv_w3, seq)
        h = _matmul_residual(mixed, w_out[l].astype(BF16), h)
        q, xn = _norm_matmul(h, norm2_w[l], w_q[l].astype(BF16), emit_xn=True)
        s1, s2, e1, e2, tau = _peer_select(q, sub_keys[l])
        wgt = _peer_weights(s1, s2, e1, e2, tau)
        peer = _peer_mlp(xn, expert_down[l].astype(BF16), expert_up[l].astype(BF16), wgt)
        if l + 1 < depth:
            h = h + peer
    return _final_norm(h, peer, norm_f_w).reshape(batch, seq, d)
```

```python
import functools

import jax
import jax.numpy as jnp
from jax import lax
from jax.experimental import pallas as pl
from jax.experimental.pallas import tpu as pltpu

F32 = jnp.float32
BF16 = jnp.bfloat16

HEAD_DIM = 64
LANES = 128
NORM_EPS = 1e-6
GN_EPS = 64e-5
DECAY_LORA = 96
ICLR_LORA = 96
GATE_LORA = 256
LORA_PAD = 512
CONV_K = 3
PEER_HEADS = 8
N_KEYS = 128
PEER_TOPK = 16
WKV_CHUNK = 128
STAT_ROWS = 8
NORM_ROWS = 128
VMEM_LIMIT = 56 * 1024 * 1024
PEER_VMEM_LIMIT = 60 * 1024 * 1024

NT_DIMS = (((1,), (1,)), ((), ()))
TN_DIMS = (((0,), (0,)), ((), ()))


def _params(*sem):
    return pltpu.CompilerParams(dimension_semantics=sem, vmem_limit_bytes=VMEM_LIMIT)


def _bdot(a, b):
    return jnp.dot(a.astype(BF16), b.astype(BF16), preferred_element_type=F32)


def _split3(x):
    hi = x.astype(BF16)
    r1 = x - hi.astype(F32)
    mid = r1.astype(BF16)
    lo = (r1 - mid.astype(F32)).astype(BF16)
    return hi, mid, lo


def _dot_exact_lhs(a_bf16, x):
    hi, mid, lo = _split3(x)
    d = lambda t: jnp.dot(a_bf16, t, preferred_element_type=F32)
    return d(hi) + d(mid) + d(lo)


def _dot_exact_rhs(x, b_bf16):
    hi, mid, lo = _split3(x)
    d = lambda t: jnp.dot(t, b_bf16, preferred_element_type=F32)
    return d(hi) + d(mid) + d(lo)


def _head_ones():
    r = lax.broadcasted_iota(jnp.int32, (LANES, LANES), 0) // HEAD_DIM
    c = lax.broadcasted_iota(jnp.int32, (LANES, LANES), 1) // HEAD_DIM
    return (r == c).astype(BF16)


def _head_sum(x):
    ones = _head_ones()
    parts = [_dot_exact_rhs(x[:, j:j + LANES], ones) for j in range(0, x.shape[1], LANES)]
    return jnp.concatenate(parts, axis=1)


def _norm_matmul_kernel(x_ref, nw_ref, w_ref, o_ref, *rest, emit_xn):
    xn_scr = rest[-1]

    @pl.when(pl.program_id(1) == 0)
    def _():
        for r0 in range(0, x_ref.shape[0], NORM_ROWS):
            x = x_ref[r0:r0 + NORM_ROWS, :]
            y = x * lax.rsqrt(jnp.mean(x * x, axis=-1, keepdims=True) + NORM_EPS)
            xn_scr[r0:r0 + NORM_ROWS, :] = (y * nw_ref[...]).astype(BF16)

    o_ref[...] = jnp.dot(xn_scr[...], w_ref[...], preferred_element_type=F32)
    if emit_xn:
        @pl.when(pl.program_id(1) == 0)
        def _():
            rest[0][...] = xn_scr[...]


def _norm_matmul(x, nw, w_bf16, *, emit_xn, tm=1024, tn=512):
    m, d = x.shape
    n = w_bf16.shape[1]
    out_shape = [jax.ShapeDtypeStruct((m, n), F32)]
    out_specs = [pl.BlockSpec((tm, tn), lambda i, j: (i, j))]
    if emit_xn:
        out_shape.append(jax.ShapeDtypeStruct((m, d), BF16))
        out_specs.append(pl.BlockSpec((tm, d), lambda i, j: (i, 0)))
    res = pl.pallas_call(
        functools.partial(_norm_matmul_kernel, emit_xn=emit_xn),
        grid=(m // tm, n // tn),
        in_specs=[pl.BlockSpec((tm, d), lambda i, j: (i, 0), pipeline_mode=pl.Buffered(1)),
                  pl.BlockSpec((1, d), lambda i, j: (0, 0)),
                  pl.BlockSpec((d, tn), lambda i, j: (0, j))],
        out_specs=out_specs,
        out_shape=out_shape,
        scratch_shapes=[pltpu.VMEM((tm, d), BF16)],
        compiler_params=_params("parallel", "arbitrary"),
        name="norm_matmul",
    )(x, nw.reshape(1, d), w_bf16)
    return res if emit_xn else res[0]


def _shift_rows(x, prev8, n, first):
    rows = lax.broadcasted_iota(jnp.int32, x.shape, 0)
    out = pltpu.roll(x, n, 0)
    for j in range(n):
        fill = jnp.where(first, 0.0, prev8[8 - n + j:8 - n + j + 1, :])
        out = jnp.where(rows == j, fill, out)
    return out


def _rwkv_prep_kernel(p_ref, pl_ref, pp_ref, ppl_ref, mu_ref, mul_ref, w0_ref, a0_ref,
                      kk_ref, ka_ref, rk_ref, w2_ref, a2_ref, g2_ref,
                      r_o, k_o, v_o, kk_o, b_o, lw_o, g_o, bonus_o, *, tiles_per_seq):
    first = (pl.program_id(0) % tiles_per_seq) == 0
    w = r_o.shape[1]
    p = p_ref[...]
    ps = p + (_shift_rows(p, pp_ref[...], 1, first) - p) * mu_ref[...]
    q = pl_ref[...]
    qs = q + (_shift_rows(q, ppl_ref[...], 1, first) - q) * mul_ref[...]
    r = ps[:, :w]
    k = ps[:, w:2 * w]
    v = ps[:, 2 * w:]
    w_log = -jax.nn.softplus(-(w0_ref[...] + _bdot(jnp.tanh(qs), w2_ref[...]))) - 0.5
    a = jax.nn.sigmoid(a0_ref[...] + _bdot(qs, a2_ref[...]))
    g = _bdot(jax.nn.sigmoid(qs), g2_ref[...])
    kk = k * kk_ref[...]
    kk = kk * lax.rsqrt(jnp.maximum(_head_sum(kk * kk), 1e-24))
    k2 = k * (1.0 + (a - 1.0) * ka_ref[...])
    r_o[...] = r
    k_o[...] = k2
    v_o[...] = v
    kk_o[...] = kk
    b_o[...] = kk * a
    lw_o[...] = -jnp.exp(w_log)
    g_o[...] = g
    bonus_o[...] = _head_sum(r * k2 * rk_ref[...]) * v


def _rwkv_prep(proj, seq, w, mu_rkv, mu_lora, w0, a0, k_k, k_a, r_k, w2p, a2p, g2p, *, tt=128):
    m = proj.shape[0]
    lora_blk = (6 * w) // LORA_PAD
    row = lambda i: (i, 0)
    prev = lambda i: (jnp.maximum(i * (tt // 8) - 1, 0), 0)
    prev_l = lambda i: (jnp.maximum(i * (tt // 8) - 1, 0), lora_blk)
    const = lambda i: (0, 0)
    vec = pl.BlockSpec((1, w), const)
    out = jax.ShapeDtypeStruct((m, w), F32)
    return pl.pallas_call(
        functools.partial(_rwkv_prep_kernel, tiles_per_seq=seq // tt),
        grid=(m // tt,),
        in_specs=[pl.BlockSpec((tt, 3 * w), row),
                  pl.BlockSpec((tt, LORA_PAD), lambda i: (i, lora_blk)),
                  pl.BlockSpec((8, 3 * w), prev),
                  pl.BlockSpec((8, LORA_PAD), prev_l),
                  pl.BlockSpec((1, 3 * w), const),
                  pl.BlockSpec((1, LORA_PAD), const),
                  vec, vec, vec, vec, vec,
                  pl.BlockSpec((LORA_PAD, w), const),
                  pl.BlockSpec((LORA_PAD, w), const),
                  pl.BlockSpec((LORA_PAD, w), const)],
        out_specs=[pl.BlockSpec((tt, w), row)] * 8,
        out_shape=[out] * 8,
        compiler_params=_params("parallel"),
        name="rwkv_prep",
    )(proj, proj, proj, proj, mu_rkv, mu_lora, w0, a0, k_k, k_a, r_k, w2p, a2p, g2p)


def _wkv_chunk(r, k, v, kk, b, lw, s_prev):
    c = WKV_CHUNK
    pairs = range(len(r))
    units = [(p, h) for p in pairs for h in range(2)]
    row = lax.broadcasted_iota(jnp.int32, (c, c), 0)
    col = lax.broadcasted_iota(jnp.int32, (c, c), 1)
    lower = row >= col
    strict = row > col
    eye = (row == col).astype(F32)
    low16 = lower.astype(BF16)
    head_a = lax.broadcasted_iota(jnp.int32, (c, LANES), 1) < HEAD_DIM
    nt = lambda x, y: lax.dot_general(x, y, NT_DIMS, preferred_element_type=F32)
    mm = lambda x, y: jnp.dot(x, y, preferred_element_type=F32)

    cum = [_dot_exact_lhs(low16, lw[p]) for p in pairs]
    cend = [cum[p][c - 1:c, :] for p in pairs]
    cc = [cum[p] - cum[p][c // 2 - 1:c // 2, :] for p in pairs]
    e_neg = [jnp.exp(-cc[p]) for p in pairs]
    at = [-kk[p] * jnp.exp(cc[p] - lw[p]) for p in pairs]
    rt = [r[p] * jnp.exp(cc[p]) for p in pairs]
    lhs = [jnp.concatenate([jnp.where(head_a, at[p], 0.0), jnp.where(head_a, 0.0, at[p]),
                            jnp.where(head_a, rt[p], 0.0), jnp.where(head_a, 0.0, rt[p])],
                           axis=0).astype(BF16) for p in pairs]
    rhs = [jnp.concatenate([b[p] * e_neg[p], k[p] * e_neg[p]], axis=0).astype(BF16) for p in pairs]
    gram = [nt(lhs[p], rhs[p]) for p in pairs]
    s0 = [s_prev[p].astype(BF16) for p in pairs]
    v16 = [v[p].astype(BF16) for p in pairs]
    state_u = [nt((-kk[p] * jnp.exp(cum[p] - lw[p])).astype(BF16), s0[p]) for p in pairs]
    state_y = [nt((r[p] * jnp.exp(cum[p])).astype(BF16), s0[p]) for p in pairs]

    pw = [jnp.where(strict, gram[p][h * c:(h + 1) * c, :c], 0.0) for p, h in units]
    inv = [eye + x for x in pw]
    for _ in range(6):
        pw16 = [x.astype(BF16) for x in pw]
        pw = [mm(x, x) for x in pw16]
        inv = [i + _bdot(i, x) for i, x in zip(inv, pw)]
    a_ak = [jnp.where(strict, gram[p][h * c:(h + 1) * c, c:], 0.0).astype(BF16) for p, h in units]
    rhs_u = [state_u[p] + mm(a_ak[2 * p + h], v16[p]) for p, h in units]
    us = [_bdot(i, x) for i, x in zip(inv, rhs_u)]
    u16 = [jnp.where(head_a, us[2 * p], us[2 * p + 1]).astype(BF16) for p in pairs]

    a_rb = [jnp.where(lower, gram[p][(2 + h) * c:(3 + h) * c, :c], 0.0).astype(BF16)
            for p, h in units]
    a_rk = [jnp.where(lower, gram[p][(2 + h) * c:(3 + h) * c, c:], 0.0).astype(BF16)
            for p, h in units]
    yh = [mm(a_rb[2 * p + h], u16[p]) + mm(a_rk[2 * p + h], v16[p]) for p, h in units]
    ys = [state_y[p] + jnp.where(head_a, yh[2 * p], yh[2 * p + 1]) for p in pairs]

    e_end = [jnp.exp(cend[p] - cum[p]) for p in pairs]
    upd = [lax.dot_general(jnp.concatenate([u16[p], v16[p]], axis=0),
                           jnp.concatenate([b[p] * e_end[p], k[p] * e_end[p]], axis=0).astype(BF16),
                           TN_DIMS, preferred_element_type=F32) for p in pairs]
    srow = lax.broadcasted_iota(jnp.int32, (LANES, LANES), 0) // HEAD_DIM
    scol = lax.broadcasted_iota(jnp.int32, (LANES, LANES), 1) // HEAD_DIM
    same_head = srow == scol
    s_new = [s_prev[p] * jnp.exp(cend[p]) + jnp.where(same_head, upd[p], 0.0) for p in pairs]
    return ys, s_new


def _wkv_kernel(r_ref, k_ref, v_ref, kk_ref, b_ref, lw_ref, y_ref, s_ref):
    @pl.when(pl.program_id(2) == 0)
    def _():
        s_ref[...] = jnp.zeros_like(s_ref)

    n_pairs = s_ref.shape[0]
    lanes = [slice(p * LANES, (p + 1) * LANES) for p in range(n_pairs)]
    load = lambda ref: [ref[:, sl] for sl in lanes]
    ys, s_new = _wkv_chunk(load(r_ref), load(k_ref), load(v_ref), load(kk_ref), load(b_ref),
                           load(lw_ref), [s_ref[p] for p in range(n_pairs)])
    for p in range(n_pairs):
        y_ref[:, lanes[p]] = ys[p]
        s_ref[p] = s_new[p]


def _wkv_scan(r, k, v, kk, b, lw, batch, seq, *, pairs=4):
    m, w = r.shape
    c = WKV_CHUNK
    nchunk = seq // c
    spec = pl.BlockSpec((c, pairs * LANES), lambda bi, p, ci: (bi * nchunk + ci, p))
    return pl.pallas_call(
        _wkv_kernel,
        grid=(batch, w // (pairs * LANES), nchunk),
        in_specs=[spec] * 6,
        out_specs=spec,
        out_shape=jax.ShapeDtypeStruct((m, w), F32),
        scratch_shapes=[pltpu.VMEM((pairs, LANES, LANES), F32)],
        compiler_params=_params("parallel", "parallel", "arbitrary"),
        name="wkv_scan",
    )(r, k, v, kk, b, lw)


def _mix_post_kernel(y_ref, g_ref, bonus_ref, lnw_ref, lnb_ref, bg_ref, cg_ref, hv_ref,
                     cgp_ref, hvp_ref, cw_ref, o_ref, *, tiles_per_seq):
    first = (pl.program_id(0) % tiles_per_seq) == 0
    w = y_ref.shape[1]
    y = y_ref[...]
    mean = _head_sum(y) * (1.0 / HEAD_DIM)
    d = y - mean
    var = _head_sum(d * d) * (1.0 / HEAD_DIM)
    yn = d * lax.rsqrt(var + GN_EPS) * lnw_ref[...] + lnb_ref[...]
    o_ref[:, :w] = ((yn + bonus_ref[...]) * g_ref[...]).astype(BF16)
    z = cg_ref[...] * hv_ref[...]
    zp = cgp_ref[...] * hvp_ref[...]
    cw = cw_ref[...]
    zc = (cw[0:1, :] * _shift_rows(z, zp, 2, first) + cw[1:2, :] * _shift_rows(z, zp, 1, first)
          + cw[2:3, :] * z)
    o_ref[:, w:] = (bg_ref[...] * zc).astype(BF16)


def _mix_post(y, g, bonus, ln_w, ln_b, proj, conv_w3, seq, *, tt=128):
    m, w = y.shape
    row = lambda i: (i, 0)
    const = lambda i: (0, 0)
    prev = lambda j: (lambda i: (jnp.maximum(i * (tt // 8) - 1, 0), j))
    return pl.pallas_call(
        functools.partial(_mix_post_kernel, tiles_per_seq=seq // tt),
        grid=(m // tt,),
        in_specs=[pl.BlockSpec((tt, w), row)] * 3
        + [pl.BlockSpec((1, w), const)] * 2
        + [pl.BlockSpec((tt, w), lambda i, j=j: (i, j)) for j in (3, 4, 5)]
        + [pl.BlockSpec((8, w), prev(4)), pl.BlockSpec((8, w), prev(5)),
           pl.BlockSpec((8, w), const)],
        out_specs=pl.BlockSpec((tt, 2 * w), row),
        out_shape=jax.ShapeDtypeStruct((m, 2 * w), BF16),
        compiler_params=_params("parallel"),
        name="mix_post",
    )(y, g, bonus, ln_w, ln_b, proj, proj, proj, proj, proj, conv_w3)


def _matmul_residual_kernel(a_ref, w_ref, res_ref, o_ref):
    o_ref[...] = res_ref[...] + jnp.dot(a_ref[...], w_ref[...], preferred_element_type=F32)


def _matmul_residual(a_bf16, w_bf16, res, *, tm=1024, tn=512):
    m, kd = a_bf16.shape
    n = w_bf16.shape[1]
    return pl.pallas_call(
        _matmul_residual_kernel,
        grid=(m // tm, n // tn),
        in_specs=[pl.BlockSpec((tm, kd), lambda i, j: (i, 0)),
                  pl.BlockSpec((kd, tn), lambda i, j: (0, j)),
                  pl.BlockSpec((tm, tn), lambda i, j: (i, j))],
        out_specs=pl.BlockSpec((tm, tn), lambda i, j: (i, j)),
        out_shape=jax.ShapeDtypeStruct((m, n), F32),
        compiler_params=_params("parallel", "arbitrary"),
        name="matmul_residual",
    )(a_bf16, w_bf16, res)


def _top_values(s, count):
    tops = []
    for _ in range(count):
        mx = jnp.max(s, axis=0, keepdims=True)
        tops.append(mx)
        s = jnp.where(s >= mx, -jnp.inf, s)
    return tops


def _peer_select_kernel(q_ref, keys_ref, s1_o, s2_o, st_o):
    q = q_ref[...]
    half = q.shape[1] // 2
    sc = []
    for c in range(2):
        qc = q[:, c * half:(c + 1) * half]
        sc.append(lax.dot_general(keys_ref[0, c].astype(BF16), qc.astype(BF16), NT_DIMS,
                                  preferred_element_type=F32))
    top1 = _top_values(sc[0], PEER_TOPK)
    top2 = _top_values(sc[1], PEER_TOPK)
    t2 = jnp.concatenate(top2, axis=0)
    cand = jnp.concatenate([a + t2 for a in top1], axis=0)
    best = _top_values(cand, PEER_TOPK)
    cmax = best[0]
    z = jnp.exp(best[0] - cmax)
    for bv in best[1:]:
        z = z + jnp.exp(bv - cmax)
    s1_o[0] = sc[0]
    s2_o[0] = sc[1]
    st_o[0] = jnp.concatenate([best[-1], top1[0], top2[0], 1.0 / z,
                               jnp.zeros((4, z.shape[1]), F32)], axis=0)


def _peer_select(q, sub_keys, *, tt=512):
    m = q.shape[0]
    heads, _, nk, half = sub_keys.shape
    big = jax.ShapeDtypeStruct((heads, nk, m), F32)
    bspec = pl.BlockSpec((1, nk, tt), lambda i, h: (h, 0, i))
    return pl.pallas_call(
        _peer_select_kernel,
        grid=(m // tt, heads),
        in_specs=[pl.BlockSpec((tt, 2 * half), lambda i, h: (i, h)),
                  pl.BlockSpec((1, 2, nk, half), lambda i, h: (h, 0, 0, 0))],
        out_specs=[bspec] * 2 + [pl.BlockSpec((1, STAT_ROWS, tt), lambda i, h: (h, 0, i))],
        out_shape=[big] * 2 + [jax.ShapeDtypeStruct((heads, STAT_ROWS, m), F32)],
        compiler_params=_params("parallel", "arbitrary"),
        name="peer_select",
    )(q, sub_keys)


def _peer_mlp_kernel(x_ref, down_ref, up_ref, s1_ref, s2_ref, st_ref, o_ref, e1_scr, e2_scr,
                     gate_scr):
    heads = s1_ref.shape[0]
    sub = 2 * N_KEYS

    @pl.when(pl.program_id(1) == 0)
    def _():
        o_ref[...] = jnp.zeros_like(o_ref)
        for h in range(heads):
            e1_scr[h] = jnp.exp(s1_ref[h] - st_ref[h, 1:2, :]) * st_ref[h, 3:4, :]
            e2_scr[h] = jnp.exp(s2_ref[h] - st_ref[h, 2:3, :])

    tm = x_ref.shape[0]
    te = down_ref.shape[0]
    first_row = pl.program_id(1) * (te // N_KEYS)

    def gate_unit(ii, t0):
        tk = slice(t0, t0 + LANES)
        acc = None
        for h in range(heads):
            s1 = s1_ref[h, pl.ds(first_row + ii, 1), :][:, tk]
            e1 = e1_scr[h, pl.ds(first_row + ii, 1), :][:, tk]
            term = jnp.where(s1 + s2_ref[h, :, tk] >= st_ref[h, 0:1, tk],
                             e1 * e2_scr[h, :, tk], 0.0)
            acc = term if acc is None else acc + term
        gate_scr[tk, ii * N_KEYS:(ii + 1) * N_KEYS] = acc.T

    for ii in range(te // N_KEYS):
        for t0 in range(0, tm, LANES):
            gate_unit(ii, t0)
    x = x_ref[...]
    hid = []
    for j in range(te // sub):
        cols = slice(j * sub, (j + 1) * sub)
        act = lax.dot_general(x, down_ref[cols, :], NT_DIMS, preferred_element_type=F32)
        gelu = 0.5 * act * (1.0 + lax.erf(act * (2.0 ** -0.5)))
        hid.append((gelu * gate_scr[:, cols]).astype(BF16))
    o_ref[...] += jnp.dot(jnp.concatenate(hid, axis=1), up_ref[...], preferred_element_type=F32)


def _peer_mlp(xn_bf16, down_bf16, up_bf16, s1, s2, stats, *, tm=512, te=512):
    m, d = xn_bf16.shape
    ne = down_bf16.shape[0]
    heads, nk, _ = s1.shape
    big = pl.BlockSpec((heads, nk, tm), lambda i, e: (0, 0, i))
    return pl.pallas_call(
        _peer_mlp_kernel,
        grid=(m // tm, ne // te),
        in_specs=[pl.BlockSpec((tm, d), lambda i, e: (i, 0)),
                  pl.BlockSpec((te, d), lambda i, e: (e, 0)),
                  pl.BlockSpec((te, d), lambda i, e: (e, 0)),
                  big, big,
                  pl.BlockSpec((heads, STAT_ROWS, tm), lambda i, e: (0, 0, i))],
        out_specs=pl.BlockSpec((tm, d), lambda i, e: (i, 0)),
        out_shape=jax.ShapeDtypeStruct((m, d), F32),
        scratch_shapes=[pltpu.VMEM((heads, nk, tm), F32), pltpu.VMEM((heads, nk, tm), F32),
                        pltpu.VMEM((tm, te), F32)],
        compiler_params=pltpu.CompilerParams(dimension_semantics=("parallel", "arbitrary"),
                                             vmem_limit_bytes=PEER_VMEM_LIMIT),
        name="peer_mlp",
    )(xn_bf16, down_bf16, up_bf16, s1, s2, stats)


def _final_norm_kernel(h_ref, p_ref, nw_ref, o_ref):
    x = h_ref[...] + p_ref[...]
    y = x * lax.rsqrt(jnp.mean(x * x, axis=-1, keepdims=True) + NORM_EPS)
    o_ref[...] = y * nw_ref[...]


def _final_norm(h, p, nw, *, tt=256):
    m, d = h.shape
    row = lambda i: (i, 0)
    return pl.pallas_call(
        _final_norm_kernel,
        grid=(m // tt,),
        in_specs=[pl.BlockSpec((tt, d), row), pl.BlockSpec((tt, d), row),
                  pl.BlockSpec((1, d), lambda i: (0, 0))],
        out_specs=pl.BlockSpec((tt, d), row),
        out_shape=jax.ShapeDtypeStruct((m, d), F32),
        compiler_params=_params("parallel"),
        name="final_norm",
    )(h, p, nw.reshape(1, d))


def _pad_rows(mat, start, total):
    return jnp.pad(mat, ((start, total - start - mat.shape[0]), (0, 0)))


def kernel(x, norm1_w, w_in, mu_shift, w0, w2, a0, a2, g2, k_k, k_a, r_k, ln_x_w, ln_x_b,
           conv_w, w_out, norm2_w, w_q, sub_keys, expert_down, expert_up, norm_f_w):
    batch, seq, d = x.shape
    depth = w_in.shape[0]
    w = w0.shape[1]
    rwkv_in = 3 * w + DECAY_LORA + ICLR_LORA + GATE_LORA
    lora_w = rwkv_in - 3 * w
    m = batch * seq
    h = x.reshape(m, d)
    row = lambda t: t.reshape(1, -1)
    for l in range(depth):
        w_cat = jnp.concatenate(
            [w_in[l][:, :3 * w], w_in[l][:, rwkv_in:],
             jnp.pad(w_in[l][:, 3 * w:rwkv_in], ((0, 0), (0, LORA_PAD - lora_w)))],
            axis=1).astype(BF16)
        proj = _norm_matmul(h, norm1_w[l], w_cat, emit_xn=False)
        mu = mu_shift[l]
        mu_lora = jnp.pad(mu[3 * w:], (0, LORA_PAD - lora_w))
        w2p = _pad_rows(w2[l], 0, LORA_PAD).astype(BF16)
        a2p = _pad_rows(a2[l], DECAY_LORA, LORA_PAD).astype(BF16)
        g2p = _pad_rows(g2[l], DECAY_LORA + ICLR_LORA, LORA_PAD).astype(BF16)
        r, k2, v, kk, bvec, lw, g, bonus = _rwkv_prep(
            proj, seq, w, row(mu[:3 * w]), row(mu_lora), row(w0[l]), row(a0[l]), row(k_k[l]),
            row(k_a[l]), row(r_k[l]), w2p, a2p, g2p)
        y = _wkv_scan(r, k2, v, kk, bvec, lw, batch, seq)
        conv_w3 = jnp.pad(conv_w[l].T, ((0, 8 - CONV_K), (0, 0)))
        mixed = _mix_post(y, g, bonus, row(ln_x_w[l]), row(ln_x_b[l]), proj, conv_w3, seq)
        h = _matmul_residual(mixed, w_out[l].astype(BF16), h)
        q, xn = _norm_matmul(h, norm2_w[l], w_q[l].astype(BF16), emit_xn=True)
        s1, s2, stats = _peer_select(q, sub_keys[l])
        peer = _peer_mlp(xn, expert_down[l].astype(BF16), expert_up[l].astype(BF16), s1, s2, stats)
        if l + 1 < depth:
            h = h + peer
    return _final_norm(h, peer, norm_f_w).reshape(batch, seq, d)
```

```python
import functools

import jax
import jax.numpy as jnp
from jax import lax
from jax.experimental import pallas as pl
from jax.experimental.pallas import tpu as pltpu

F32 = jnp.float32
BF16 = jnp.bfloat16

HEAD_DIM = 64
LANES = 128
NORM_EPS = 1e-6
GN_EPS = 64e-5
DECAY_LORA = 96
ICLR_LORA = 96
GATE_LORA = 256
LORA_PAD = 512
CONV_K = 3
PEER_HEADS = 8
N_KEYS = 128
PEER_TOPK = 16
WKV_CHUNK = 128
STAT_ROWS = 8
NORM_ROWS = 128
VMEM_LIMIT = 56 * 1024 * 1024
PEER_VMEM_LIMIT = 60 * 1024 * 1024

NT_DIMS = (((1,), (1,)), ((), ()))
TN_DIMS = (((0,), (0,)), ((), ()))


def _params(*sem):
    return pltpu.CompilerParams(dimension_semantics=sem, vmem_limit_bytes=VMEM_LIMIT)


def _bdot(a, b):
    return jnp.dot(a.astype(BF16), b.astype(BF16), preferred_element_type=F32)


def _split3(x):
    hi = x.astype(BF16)
    r1 = x - hi.astype(F32)
    mid = r1.astype(BF16)
    lo = (r1 - mid.astype(F32)).astype(BF16)
    return hi, mid, lo


def _dot_exact_lhs(a_bf16, x):
    hi, mid, lo = _split3(x)
    d = lambda t: jnp.dot(a_bf16, t, preferred_element_type=F32)
    return d(hi) + d(mid) + d(lo)


def _dot_exact_rhs(x, b_bf16):
    hi, mid, lo = _split3(x)
    d = lambda t: jnp.dot(t, b_bf16, preferred_element_type=F32)
    return d(hi) + d(mid) + d(lo)


def _head_ones():
    r = lax.broadcasted_iota(jnp.int32, (LANES, LANES), 0) // HEAD_DIM
    c = lax.broadcasted_iota(jnp.int32, (LANES, LANES), 1) // HEAD_DIM
    return (r == c).astype(BF16)


def _head_sum(x):
    ones = _head_ones()
    parts = [_dot_exact_rhs(x[:, j:j + LANES], ones) for j in range(0, x.shape[1], LANES)]
    return jnp.concatenate(parts, axis=1)


def _norm_matmul_kernel(x_ref, nw_ref, w_ref, o_ref, *rest, emit_xn):
    xn_scr = rest[-1]

    @pl.when(pl.program_id(1) == 0)
    def _():
        for r0 in range(0, x_ref.shape[0], NORM_ROWS):
            x = x_ref[r0:r0 + NORM_ROWS, :]
            y = x * lax.rsqrt(jnp.mean(x * x, axis=-1, keepdims=True) + NORM_EPS)
            xn_scr[r0:r0 + NORM_ROWS, :] = (y * nw_ref[...]).astype(BF16)

    o_ref[...] = jnp.dot(xn_scr[...], w_ref[...].astype(BF16), preferred_element_type=F32)
    if emit_xn:
        @pl.when(pl.program_id(1) == 0)
        def _():
            rest[0][...] = xn_scr[...]


def _norm_matmul(x, nw, w, *, emit_xn, tm, tn=512):
    m, d = x.shape
    n = w.shape[1]
    out_shape = [jax.ShapeDtypeStruct((m, n), F32)]
    out_specs = [pl.BlockSpec((tm, tn), lambda i, j: (i, j))]
    if emit_xn:
        out_shape.append(jax.ShapeDtypeStruct((m, d), BF16))
        out_specs.append(pl.BlockSpec((tm, d), lambda i, j: (i, 0)))
    res = pl.pallas_call(
        functools.partial(_norm_matmul_kernel, emit_xn=emit_xn),
        grid=(m // tm, pl.cdiv(n, tn)),
        in_specs=[pl.BlockSpec((tm, d), lambda i, j: (i, 0), pipeline_mode=pl.Buffered(1)),
                  pl.BlockSpec((1, d), lambda i, j: (0, 0)),
                  pl.BlockSpec((d, tn), lambda i, j: (0, j))],
        out_specs=out_specs,
        out_shape=out_shape,
        scratch_shapes=[pltpu.VMEM((tm, d), BF16)],
        compiler_params=_params("parallel", "arbitrary"),
        name="norm_matmul",
    )(x, nw.reshape(1, d), w)
    return res if emit_xn else res[0]


def _shift_rows(x, prev8, n, first):
    rows = lax.broadcasted_iota(jnp.int32, x.shape, 0)
    out = pltpu.roll(x, n, 0)
    for j in range(n):
        fill = jnp.where(first, 0.0, prev8[8 - n + j:8 - n + j + 1, :])
        out = jnp.where(rows == j, fill, out)
    return out


def _rwkv_prep_kernel(p_ref, pl_ref, pp_ref, ppl_ref, mu_ref, mul_ref, w0_ref, a0_ref,
                      kk_ref, ka_ref, rk_ref, w2_ref, a2_ref, g2_ref,
                      r_o, k_o, v_o, kk_o, b_o, lw_o, g_o, bonus_o, *, tiles_per_seq):
    first = (pl.program_id(0) % tiles_per_seq) == 0
    w = r_o.shape[1]
    p = p_ref[...]
    ps = p + (_shift_rows(p, pp_ref[...], 1, first) - p) * mu_ref[...]
    q = pl_ref[...]
    qs = q + (_shift_rows(q, ppl_ref[...], 1, first) - q) * mul_ref[...]
    r = ps[:, :w]
    k = ps[:, w:2 * w]
    v = ps[:, 2 * w:]
    w_log = -jax.nn.softplus(-(w0_ref[...] + _bdot(jnp.tanh(qs), w2_ref[...]))) - 0.5
    a = jax.nn.sigmoid(a0_ref[...] + _bdot(qs, a2_ref[...]))
    g = _bdot(jax.nn.sigmoid(qs), g2_ref[...])
    kk = k * kk_ref[...]
    kk = kk * lax.rsqrt(jnp.maximum(_head_sum(kk * kk), 1e-24))
    k2 = k * (1.0 + (a - 1.0) * ka_ref[...])
    r_o[...] = r
    k_o[...] = k2
    v_o[...] = v
    kk_o[...] = kk
    b_o[...] = kk * a
    lw_o[...] = -jnp.exp(w_log)
    g_o[...] = g
    bonus_o[...] = _head_sum(r * k2 * rk_ref[...]) * v


def _rwkv_prep(proj, seq, w, mu_rkv, mu_lora, w0, a0, k_k, k_a, r_k, w2p, a2p, g2p, *, tt=128):
    m = proj.shape[0]
    lora_blk = (3 * w) // LORA_PAD
    row = lambda i: (i, 0)
    prev = lambda i: (jnp.maximum(i * (tt // 8) - 1, 0), 0)
    prev_l = lambda i: (jnp.maximum(i * (tt // 8) - 1, 0), lora_blk)
    const = lambda i: (0, 0)
    vec = pl.BlockSpec((1, w), const)
    out = jax.ShapeDtypeStruct((m, w), F32)
    return pl.pallas_call(
        functools.partial(_rwkv_prep_kernel, tiles_per_seq=seq // tt),
        grid=(m // tt,),
        in_specs=[pl.BlockSpec((tt, 3 * w), row),
                  pl.BlockSpec((tt, LORA_PAD), lambda i: (i, lora_blk)),
                  pl.BlockSpec((8, 3 * w), prev),
                  pl.BlockSpec((8, LORA_PAD), prev_l),
                  pl.BlockSpec((1, 3 * w), const),
                  pl.BlockSpec((1, LORA_PAD), const),
                  vec, vec, vec, vec, vec,
                  pl.BlockSpec((LORA_PAD, w), const),
                  pl.BlockSpec((LORA_PAD, w), const),
                  pl.BlockSpec((LORA_PAD, w), const)],
        out_specs=[pl.BlockSpec((tt, w), row)] * 8,
        out_shape=[out] * 8,
        compiler_params=_params("parallel"),
        name="rwkv_prep",
    )(proj, proj, proj, proj, mu_rkv, mu_lora, w0, a0, k_k, k_a, r_k, w2p, a2p, g2p)


def _wkv_chunk(r, k, v, kk, b, lw, s_prev):
    c = WKV_CHUNK
    pairs = range(len(r))
    units = [(p, h) for p in pairs for h in range(2)]
    row = lax.broadcasted_iota(jnp.int32, (c, c), 0)
    col = lax.broadcasted_iota(jnp.int32, (c, c), 1)
    lower = row >= col
    strict = row > col
    eye = (row == col).astype(F32)
    low16 = lower.astype(BF16)
    head_a = lax.broadcasted_iota(jnp.int32, (c, LANES), 1) < HEAD_DIM
    nt = lambda x, y: lax.dot_general(x, y, NT_DIMS, preferred_element_type=F32)
    mm = lambda x, y: jnp.dot(x, y, preferred_element_type=F32)

    cum = [_dot_exact_lhs(low16, lw[p]) for p in pairs]
    cend = [cum[p][c - 1:c, :] for p in pairs]
    cc = [cum[p] - cum[p][c // 2 - 1:c // 2, :] for p in pairs]
    e_neg = [jnp.exp(-cc[p]) for p in pairs]
    at = [-kk[p] * jnp.exp(cc[p] - lw[p]) for p in pairs]
    rt = [r[p] * jnp.exp(cc[p]) for p in pairs]
    lhs = [jnp.concatenate([jnp.where(head_a, at[p], 0.0), jnp.where(head_a, 0.0, at[p]),
                            jnp.where(head_a, rt[p], 0.0), jnp.where(head_a, 0.0, rt[p])],
                           axis=0).astype(BF16) for p in pairs]
    rhs = [jnp.concatenate([b[p] * e_neg[p], k[p] * e_neg[p]], axis=0).astype(BF16) for p in pairs]
    gram = [nt(lhs[p], rhs[p]) for p in pairs]
    s0 = [s_prev[p].astype(BF16) for p in pairs]
    v16 = [v[p].astype(BF16) for p in pairs]
    state_u = [nt((-kk[p] * jnp.exp(cum[p] - lw[p])).astype(BF16), s0[p]) for p in pairs]
    state_y = [nt((r[p] * jnp.exp(cum[p])).astype(BF16), s0[p]) for p in pairs]

    pw = [jnp.where(strict, gram[p][h * c:(h + 1) * c, :c], 0.0) for p, h in units]
    inv = [eye + x for x in pw]
    for _ in range(6):
        pw16 = [x.astype(BF16) for x in pw]
        pw = [mm(x, x) for x in pw16]
        inv = [i + _bdot(i, x) for i, x in zip(inv, pw)]
    a_ak = [jnp.where(strict, gram[p][h * c:(h + 1) * c, c:], 0.0).astype(BF16) for p, h in units]
    rhs_u = [state_u[p] + mm(a_ak[2 * p + h], v16[p]) for p, h in units]
    us = [_bdot(i, x) for i, x in zip(inv, rhs_u)]
    u16 = [jnp.where(head_a, us[2 * p], us[2 * p + 1]).astype(BF16) for p in pairs]

    a_rb = [jnp.where(lower, gram[p][(2 + h) * c:(3 + h) * c, :c], 0.0).astype(BF16)
            for p, h in units]
    a_rk = [jnp.where(lower, gram[p][(2 + h) * c:(3 + h) * c, c:], 0.0).astype(BF16)
            for p, h in units]
    yh = [mm(a_rb[2 * p + h], u16[p]) + mm(a_rk[2 * p + h], v16[p]) for p, h in units]
    ys = [state_y[p] + jnp.where(head_a, yh[2 * p], yh[2 * p + 1]) for p in pairs]

    e_end = [jnp.exp(cend[p] - cum[p]) for p in pairs]
    upd = [lax.dot_general(jnp.concatenate([u16[p], v16[p]], axis=0),
                           jnp.concatenate([b[p] * e_end[p], k[p] * e_end[p]], axis=0).astype(BF16),
                           TN_DIMS, preferred_element_type=F32) for p in pairs]
    srow = lax.broadcasted_iota(jnp.int32, (LANES, LANES), 0) // HEAD_DIM
    scol = lax.broadcasted_iota(jnp.int32, (LANES, LANES), 1) // HEAD_DIM
    same_head = srow == scol
    s_new = [s_prev[p] * jnp.exp(cend[p]) + jnp.where(same_head, upd[p], 0.0) for p in pairs]
    return ys, s_new


def _wkv_kernel(r_ref, k_ref, v_ref, kk_ref, b_ref, lw_ref, y_ref, s_ref):
    @pl.when(pl.program_id(2) == 0)
    def _():
        s_ref[...] = jnp.zeros_like(s_ref)

    n_pairs = s_ref.shape[0]
    lanes = [slice(p * LANES, (p + 1) * LANES) for p in range(n_pairs)]
    load = lambda ref: [ref[:, sl] for sl in lanes]
    ys, s_new = _wkv_chunk(load(r_ref), load(k_ref), load(v_ref), load(kk_ref), load(b_ref),
                           load(lw_ref), [s_ref[p] for p in range(n_pairs)])
    for p in range(n_pairs):
        y_ref[:, lanes[p]] = ys[p]
        s_ref[p] = s_new[p]


def _wkv_scan(r, k, v, kk, b, lw, batch, seq, *, pairs=4):
    m, w = r.shape
    c = WKV_CHUNK
    nchunk = seq // c
    spec = pl.BlockSpec((c, pairs * LANES), lambda bi, p, ci: (bi * nchunk + ci, p))
    return pl.pallas_call(
        _wkv_kernel,
        grid=(batch, w // (pairs * LANES), nchunk),
        in_specs=[spec] * 6,
        out_specs=spec,
        out_shape=jax.ShapeDtypeStruct((m, w), F32),
        scratch_shapes=[pltpu.VMEM((pairs, LANES, LANES), F32)],
        compiler_params=_params("parallel", "parallel", "arbitrary"),
        name="wkv_scan",
    )(r, k, v, kk, b, lw)


def _mix_post_kernel(y_ref, g_ref, bonus_ref, lnw_ref, lnb_ref, bg_ref, cg_ref, hv_ref,
                     cgp_ref, hvp_ref, cw_ref, o_ref, *, tiles_per_seq, lane_pad):
    first = (pl.program_id(0) % tiles_per_seq) == 0
    w = y_ref.shape[1]
    y = y_ref[...]
    mean = _head_sum(y) * (1.0 / HEAD_DIM)
    d = y - mean
    var = _head_sum(d * d) * (1.0 / HEAD_DIM)
    yn = d * lax.rsqrt(var + GN_EPS) * lnw_ref[...] + lnb_ref[...]
    o_ref[:, :w] = ((yn + bonus_ref[...]) * g_ref[...]).astype(BF16)
    z = cg_ref[...] * hv_ref[...]
    zp = cgp_ref[...] * hvp_ref[...]
    cw = cw_ref[...]
    zc = (cw[0:1, :] * _shift_rows(z, zp, 2, first) + cw[1:2, :] * _shift_rows(z, zp, 1, first)
          + cw[2:3, :] * z)
    o_ref[:, w:] = (bg_ref[...] * zc)[:, lane_pad:lane_pad + w].astype(BF16)


def _mix_post(y, g, bonus, ln_w, ln_b, proj, conv_w, conv_start, seq, *, tt=128):
    m, w = y.shape
    lane_pad = conv_start % LANES
    win = w + (LANES if lane_pad else 0)
    starts = [conv_start - lane_pad + j * w for j in range(3)]
    cw = jnp.pad(conv_w.T, ((0, 8 - CONV_K), (lane_pad, win - w - lane_pad)))
    row = lambda i: (i, 0)
    const = lambda i: (0, 0)
    cur = lambda s: pl.BlockSpec((pl.Element(tt), pl.Element(win)), lambda i: (i * tt, s))
    prev = lambda s: pl.BlockSpec((pl.Element(8), pl.Element(win)),
                                  lambda i: (pl.multiple_of(jnp.maximum(i * tt - 8, 0), 8), s))
    return pl.pallas_call(
        functools.partial(_mix_post_kernel, tiles_per_seq=seq // tt, lane_pad=lane_pad),
        grid=(m // tt,),
        in_specs=[pl.BlockSpec((tt, w), row)] * 3
        + [pl.BlockSpec((1, w), const)] * 2
        + [cur(s) for s in starts]
        + [prev(starts[1]), prev(starts[2]), pl.BlockSpec((8, win), const)],
        out_specs=pl.BlockSpec((tt, 2 * w), row),
        out_shape=jax.ShapeDtypeStruct((m, 2 * w), BF16),
        compiler_params=_params("parallel"),
        name="mix_post",
    )(y, g, bonus, ln_w, ln_b, proj, proj, proj, proj, proj, cw)


def _matmul_residual_kernel(a_ref, w_ref, res_ref, o_ref):
    o_ref[...] = res_ref[...] + jnp.dot(a_ref[...], w_ref[...].astype(BF16),
                                        preferred_element_type=F32)


def _matmul_residual(a_bf16, w, res, *, tm=1024, tn=512):
    m, kd = a_bf16.shape
    n = w.shape[1]
    return pl.pallas_call(
        _matmul_residual_kernel,
        grid=(m // tm, n // tn),
        in_specs=[pl.BlockSpec((tm, kd), lambda i, j: (i, 0)),
                  pl.BlockSpec((kd, tn), lambda i, j: (0, j)),
                  pl.BlockSpec((tm, tn), lambda i, j: (i, j))],
        out_specs=pl.BlockSpec((tm, tn), lambda i, j: (i, j)),
        out_shape=jax.ShapeDtypeStruct((m, n), F32),
        compiler_params=_params("parallel", "arbitrary"),
        name="matmul_residual",
    )(a_bf16, w, res)


def _top_values(s, count):
    tops = []
    for _ in range(count):
        mx = jnp.max(s, axis=0, keepdims=True)
        tops.append(mx)
        s = jnp.where(s >= mx, -jnp.inf, s)
    return tops


def _peer_select_kernel(q_ref, keys_ref, s1_o, s2_o, st_o):
    q = q_ref[...]
    half = q.shape[1] // 2
    sc = []
    for c in range(2):
        qc = q[:, c * half:(c + 1) * half]
        sc.append(lax.dot_general(keys_ref[0, c].astype(BF16), qc.astype(BF16), NT_DIMS,
                                  preferred_element_type=F32))
    top1 = _top_values(sc[0], PEER_TOPK)
    top2 = _top_values(sc[1], PEER_TOPK)
    t1 = jnp.concatenate(top1, axis=0)
    t2 = jnp.concatenate(top2, axis=0)
    row8 = lax.broadcasted_iota(jnp.int32, (8, t2.shape[1]), 0)
    groups = [top1[0] + t2]
    for m_ in range(1, 8):
        n_keep = PEER_TOPK // (m_ + 1)
        g8 = top1[m_] + t2[:8]
        groups.append(g8 if n_keep >= 8 else jnp.where(row8 < n_keep, g8, -jnp.inf))
    groups.append(t1[8:] + top2[0])
    cand = jnp.concatenate(groups, axis=0)
    best = _top_values(cand, PEER_TOPK)
    cmax = best[0]
    z = jnp.exp(best[0] - cmax)
    for bv in best[1:]:
        z = z + jnp.exp(bv - cmax)
    s1_o[0] = sc[0]
    s2_o[0] = sc[1]
    st_o[0] = jnp.concatenate([best[-1], top1[0], top2[0], 1.0 / z,
                               jnp.zeros((4, z.shape[1]), F32)], axis=0)


def _peer_select(q, sub_keys, *, tt=512):
    assert PEER_TOPK == 16, "candidate groups in _peer_select_kernel are laid out for K = 16"
    m = q.shape[0]
    heads, _, nk, half = sub_keys.shape
    big = jax.ShapeDtypeStruct((heads, nk, m), F32)
    bspec = pl.BlockSpec((1, nk, tt), lambda i, h: (h, 0, i))
    return pl.pallas_call(
        _peer_select_kernel,
        grid=(m // tt, heads),
        in_specs=[pl.BlockSpec((tt, 2 * half), lambda i, h: (i, h)),
                  pl.BlockSpec((1, 2, nk, half), lambda i, h: (h, 0, 0, 0))],
        out_specs=[bspec] * 2 + [pl.BlockSpec((1, STAT_ROWS, tt), lambda i, h: (h, 0, i))],
        out_shape=[big] * 2 + [jax.ShapeDtypeStruct((heads, STAT_ROWS, m), F32)],
        compiler_params=_params("parallel", "arbitrary"),
        name="peer_select",
    )(q, sub_keys)


def _peer_mlp_kernel(x_ref, down_ref, up_ref, s1_ref, s2_ref, st_ref, o_ref, e1_scr, e2_scr,
                     gate_scr):
    heads = s1_ref.shape[0]
    sub = 2 * N_KEYS

    @pl.when(pl.program_id(1) == 0)
    def _():
        o_ref[...] = jnp.zeros_like(o_ref)
        for h in range(heads):
            e1_scr[h] = jnp.exp(s1_ref[h] - st_ref[h, 1:2, :]) * st_ref[h, 3:4, :]
            e2_scr[h] = jnp.exp(s2_ref[h] - st_ref[h, 2:3, :])

    tm = x_ref.shape[0]
    te = down_ref.shape[0]
    first_row = pl.program_id(1) * (te // N_KEYS)

    def gate_unit(ii, t0):
        tk = slice(t0, t0 + LANES)
        acc = None
        for h in range(heads):
            s1 = s1_ref[h, pl.ds(first_row + ii, 1), :][:, tk]
            e1 = e1_scr[h, pl.ds(first_row + ii, 1), :][:, tk]
            term = jnp.where(s1 + s2_ref[h, :, tk] >= st_ref[h, 0:1, tk],
                             e1 * e2_scr[h, :, tk], 0.0)
            acc = term if acc is None else acc + term
        gate_scr[tk, ii * N_KEYS:(ii + 1) * N_KEYS] = acc.T

    for ii in range(te // N_KEYS):
        for t0 in range(0, tm, LANES):
            gate_unit(ii, t0)
    x = x_ref[...]
    hid = []
    for j in range(te // sub):
        cols = slice(j * sub, (j + 1) * sub)
        act = lax.dot_general(x, down_ref[cols, :], NT_DIMS, preferred_element_type=F32)
        gelu = 0.5 * act * (1.0 + lax.erf(act * (2.0 ** -0.5)))
        hid.append((gelu * gate_scr[:, cols]).astype(BF16))
    o_ref[...] += jnp.dot(jnp.concatenate(hid, axis=1), up_ref[...], preferred_element_type=F32)


def _peer_mlp(xn_bf16, down_bf16, up_bf16, s1, s2, stats, *, tm=512, te=512):
    m, d = xn_bf16.shape
    ne = down_bf16.shape[0]
    heads, nk, _ = s1.shape
    big = pl.BlockSpec((heads, nk, tm), lambda i, e: (0, 0, i))
    return pl.pallas_call(
        _peer_mlp_kernel,
        grid=(m // tm, ne // te),
        in_specs=[pl.BlockSpec((tm, d), lambda i, e: (i, 0)),
                  pl.BlockSpec((te, d), lambda i, e: (e, 0)),
                  pl.BlockSpec((te, d), lambda i, e: (e, 0)),
                  big, big,
                  pl.BlockSpec((heads, STAT_ROWS, tm), lambda i, e: (0, 0, i))],
        out_specs=pl.BlockSpec((tm, d), lambda i, e: (i, 0)),
        out_shape=jax.ShapeDtypeStruct((m, d), F32),
        scratch_shapes=[pltpu.VMEM((heads, nk, tm), F32), pltpu.VMEM((heads, nk, tm), F32),
                        pltpu.VMEM((tm, te), F32)],
        compiler_params=pltpu.CompilerParams(dimension_semantics=("parallel", "arbitrary"),
                                             vmem_limit_bytes=PEER_VMEM_LIMIT),
        name="peer_mlp",
    )(xn_bf16, down_bf16, up_bf16, s1, s2, stats)


def _final_norm_kernel(h_ref, p_ref, nw_ref, o_ref):
    x = h_ref[...] + p_ref[...]
    y = x * lax.rsqrt(jnp.mean(x * x, axis=-1, keepdims=True) + NORM_EPS)
    o_ref[...] = y * nw_ref[...]


def _final_norm(h, p, nw, *, tt=256):
    m, d = h.shape
    row = lambda i: (i, 0)
    return pl.pallas_call(
        _final_norm_kernel,
        grid=(m // tt,),
        in_specs=[pl.BlockSpec((tt, d), row), pl.BlockSpec((tt, d), row),
                  pl.BlockSpec((1, d), lambda i: (0, 0))],
        out_specs=pl.BlockSpec((tt, d), row),
        out_shape=jax.ShapeDtypeStruct((m, d), F32),
        compiler_params=_params("parallel"),
        name="final_norm",
    )(h, p, nw.reshape(1, d))


def _pad_rows(mat, start, total):
    return jnp.pad(mat, ((start, total - start - mat.shape[0]), (0, 0)))


def kernel(x, norm1_w, w_in, mu_shift, w0, w2, a0, a2, g2, k_k, k_a, r_k, ln_x_w, ln_x_b,
           conv_w, w_out, norm2_w, w_q, sub_keys, expert_down, expert_up, norm_f_w):
    batch, seq, d = x.shape
    depth = w_in.shape[0]
    w = w0.shape[1]
    rwkv_in = 3 * w + DECAY_LORA + ICLR_LORA + GATE_LORA
    lora_w = rwkv_in - 3 * w
    m = batch * seq
    h = x.reshape(m, d)
    row = lambda t: t.reshape(1, -1)
    for l in range(depth):
        proj = _norm_matmul(h, norm1_w[l], w_in[l], emit_xn=False, tm=1024)
        mu = mu_shift[l]
        mu_lora = jnp.pad(mu[3 * w:], (0, LORA_PAD - lora_w))
        w2p = _pad_rows(w2[l], 0, LORA_PAD).astype(BF16)
        a2p = _pad_rows(a2[l], DECAY_LORA, LORA_PAD).astype(BF16)
        g2p = _pad_rows(g2[l], DECAY_LORA + ICLR_LORA, LORA_PAD).astype(BF16)
        r, k2, v, kk, bvec, lw, g, bonus = _rwkv_prep(
            proj, seq, w, row(mu[:3 * w]), row(mu_lora), row(w0[l]), row(a0[l]), row(k_k[l]),
            row(k_a[l]), row(r_k[l]), w2p, a2p, g2p)
        y = _wkv_scan(r, k2, v, kk, bvec, lw, batch, seq)
        mixed = _mix_post(y, g, bonus, row(ln_x_w[l]), row(ln_x_b[l]), proj, conv_w[l], rwkv_in,
                          seq)
        h = _matmul_residual(mixed, w_out[l], h)
        q, xn = _norm_matmul(h, norm2_w[l], w_q[l], emit_xn=True, tm=512)
        s1, s2, stats = _peer_select(q, sub_keys[l])
        peer = _peer_mlp(xn, expert_down[l].astype(BF16), expert_up[l].astype(BF16), s1, s2, stats)
        if l + 1 < depth:
            h = h + peer
    return _final_norm(h, peer, norm_f_w).reshape(batch, seq, d)
```

```python
import functools

import jax
import jax.numpy as jnp
from jax import lax
from jax.experimental import pallas as pl
from jax.experimental.pallas import tpu as pltpu

F32 = jnp.float32
BF16 = jnp.bfloat16

HEAD_DIM = 64
LANES = 128
NORM_EPS = 1e-6
GN_EPS = 64e-5
DECAY_LORA = 96
ICLR_LORA = 96
GATE_LORA = 256
LORA_PAD = 512
CONV_K = 3
PEER_HEADS = 8
N_KEYS = 128
PEER_TOPK = 16
WKV_CHUNK = 128
STAT_ROWS = 8
NORM_ROWS = 128
VMEM_LIMIT = 56 * 1024 * 1024
PEER_VMEM_LIMIT = 60 * 1024 * 1024

NT_DIMS = (((1,), (1,)), ((), ()))
TN_DIMS = (((0,), (0,)), ((), ()))


def _params(*sem):
    return pltpu.CompilerParams(dimension_semantics=sem, vmem_limit_bytes=VMEM_LIMIT)


def _bdot(a, b):
    return jnp.dot(a.astype(BF16), b.astype(BF16), preferred_element_type=F32)


def _split3(x):
    hi = x.astype(BF16)
    r1 = x - hi.astype(F32)
    mid = r1.astype(BF16)
    lo = (r1 - mid.astype(F32)).astype(BF16)
    return hi, mid, lo


def _dot_exact_lhs(a_bf16, x):
    hi, mid, lo = _split3(x)
    d = lambda t: jnp.dot(a_bf16, t, preferred_element_type=F32)
    return d(hi) + d(mid) + d(lo)


def _dot_exact_rhs(x, b_bf16):
    hi, mid, lo = _split3(x)
    d = lambda t: jnp.dot(t, b_bf16, preferred_element_type=F32)
    return d(hi) + d(mid) + d(lo)


def _head_ones():
    r = lax.broadcasted_iota(jnp.int32, (LANES, LANES), 0) // HEAD_DIM
    c = lax.broadcasted_iota(jnp.int32, (LANES, LANES), 1) // HEAD_DIM
    return (r == c).astype(BF16)


def _head_sum(x):
    ones = _head_ones()
    parts = [_dot_exact_rhs(x[:, j:j + LANES], ones) for j in range(0, x.shape[1], LANES)]
    return jnp.concatenate(parts, axis=1)


def _norm_matmul_kernel(x_ref, nw_ref, w_ref, o_ref, *rest, emit_xn):
    xn_scr = rest[-1]

    @pl.when(pl.program_id(1) == 0)
    def _():
        for r0 in range(0, x_ref.shape[0], NORM_ROWS):
            x = x_ref[r0:r0 + NORM_ROWS, :]
            y = x * lax.rsqrt(jnp.mean(x * x, axis=-1, keepdims=True) + NORM_EPS)
            xn_scr[r0:r0 + NORM_ROWS, :] = (y * nw_ref[...]).astype(BF16)

    o_ref[...] = jnp.dot(xn_scr[...], w_ref[...].astype(BF16), preferred_element_type=F32)
    if emit_xn:
        @pl.when(pl.program_id(1) == 0)
        def _():
            rest[0][...] = xn_scr[...]


def _norm_matmul(x, nw, w_stack, layer, *, emit_xn, tm, tn=512):
    m, d = x.shape
    n = w_stack.shape[2]
    out_shape = [jax.ShapeDtypeStruct((m, n), F32)]
    out_specs = [pl.BlockSpec((tm, tn), lambda i, j: (i, j))]
    if emit_xn:
        out_shape.append(jax.ShapeDtypeStruct((m, d), BF16))
        out_specs.append(pl.BlockSpec((tm, d), lambda i, j: (i, 0)))
    res = pl.pallas_call(
        functools.partial(_norm_matmul_kernel, emit_xn=emit_xn),
        grid=(m // tm, pl.cdiv(n, tn)),
        in_specs=[pl.BlockSpec((tm, d), lambda i, j: (i, 0), pipeline_mode=pl.Buffered(1)),
                  pl.BlockSpec((1, d), lambda i, j: (0, 0)),
                  pl.BlockSpec((None, d, tn), lambda i, j: (layer, 0, j))],
        out_specs=out_specs,
        out_shape=out_shape,
        scratch_shapes=[pltpu.VMEM((tm, d), BF16)],
        compiler_params=_params("parallel", "arbitrary"),
        name="norm_matmul",
    )(x, nw.reshape(1, d), w_stack)
    return res if emit_xn else res[0]


def _shift_rows(x, prev8, n, first):
    rows = lax.broadcasted_iota(jnp.int32, x.shape, 0)
    out = pltpu.roll(x, n, 0)
    for j in range(n):
        fill = jnp.where(first, 0.0, prev8[8 - n + j:8 - n + j + 1, :])
        out = jnp.where(rows == j, fill, out)
    return out


def _rwkv_prep_kernel(p_ref, pl_ref, pp_ref, ppl_ref, mu_ref, mul_ref, w0_ref, a0_ref,
                      kk_ref, ka_ref, rk_ref, w2_ref, a2_ref, g2_ref,
                      r_o, k_o, v_o, kk_o, b_o, lw_o, g_o, bonus_o, *, tiles_per_seq):
    first = (pl.program_id(0) % tiles_per_seq) == 0
    w = r_o.shape[1]
    p = p_ref[...]
    ps = p + (_shift_rows(p, pp_ref[...], 1, first) - p) * mu_ref[...]
    q = pl_ref[...]
    qs = q + (_shift_rows(q, ppl_ref[...], 1, first) - q) * mul_ref[...]
    r = ps[:, :w]
    k = ps[:, w:2 * w]
    v = ps[:, 2 * w:]
    w_log = -jax.nn.softplus(-(w0_ref[...] + _bdot(jnp.tanh(qs), w2_ref[...]))) - 0.5
    a = jax.nn.sigmoid(a0_ref[...] + _bdot(qs, a2_ref[...]))
    g = _bdot(jax.nn.sigmoid(qs), g2_ref[...])
    kk = k * kk_ref[...]
    kk = kk * lax.rsqrt(jnp.maximum(_head_sum(kk * kk), 1e-24))
    k2 = k * (1.0 + (a - 1.0) * ka_ref[...])
    r_o[...] = r.astype(BF16)
    k_o[...] = k2.astype(BF16)
    v_o[...] = v.astype(BF16)
    kk_o[...] = kk.astype(BF16)
    b_o[...] = (kk * a).astype(BF16)
    lw_o[...] = -jnp.exp(w_log)
    g_o[...] = g.astype(BF16)
    bonus_o[...] = (_head_sum(r * k2 * rk_ref[...]) * v).astype(BF16)


def _rwkv_prep(proj, seq, w, mu_rkv, mu_lora, w0, a0, k_k, k_a, r_k, w2p, a2p, g2p, *, tt=128):
    m = proj.shape[0]
    lora_blk = (3 * w) // LORA_PAD
    row = lambda i: (i, 0)
    prev = lambda i: (jnp.maximum(i * (tt // 8) - 1, 0), 0)
    prev_l = lambda i: (jnp.maximum(i * (tt // 8) - 1, 0), lora_blk)
    const = lambda i: (0, 0)
    vec = pl.BlockSpec((1, w), const)
    out = jax.ShapeDtypeStruct((m, w), F32)
    out16 = jax.ShapeDtypeStruct((m, w), BF16)
    return pl.pallas_call(
        functools.partial(_rwkv_prep_kernel, tiles_per_seq=seq // tt),
        grid=(m // tt,),
        in_specs=[pl.BlockSpec((tt, 3 * w), row),
                  pl.BlockSpec((tt, LORA_PAD), lambda i: (i, lora_blk)),
                  pl.BlockSpec((8, 3 * w), prev),
                  pl.BlockSpec((8, LORA_PAD), prev_l),
                  pl.BlockSpec((1, 3 * w), const),
                  pl.BlockSpec((1, LORA_PAD), const),
                  vec, vec, vec, vec, vec,
                  pl.BlockSpec((LORA_PAD, w), const),
                  pl.BlockSpec((LORA_PAD, w), const),
                  pl.BlockSpec((LORA_PAD, w), const)],
        out_specs=[pl.BlockSpec((tt, w), row)] * 8,
        out_shape=[out16] * 5 + [out, out16, out16],
        compiler_params=_params("parallel"),
        name="rwkv_prep",
    )(proj, proj, proj, proj, mu_rkv, mu_lora, w0, a0, k_k, k_a, r_k, w2p, a2p, g2p)


def _wkv_chunk(r, k, v, kk, b, lw, s_prev):
    c = WKV_CHUNK
    pairs = range(len(r))
    units = [(p, h) for p in pairs for h in range(2)]
    row = lax.broadcasted_iota(jnp.int32, (c, c), 0)
    col = lax.broadcasted_iota(jnp.int32, (c, c), 1)
    lower = row >= col
    strict = row > col
    eye = (row == col).astype(F32)
    low16 = lower.astype(BF16)
    head_a = lax.broadcasted_iota(jnp.int32, (c, LANES), 1) < HEAD_DIM
    nt = lambda x, y: lax.dot_general(x, y, NT_DIMS, preferred_element_type=F32)
    mm = lambda x, y: jnp.dot(x, y, preferred_element_type=F32)

    cum = [_dot_exact_lhs(low16, lw[p]) for p in pairs]
    cend = [cum[p][c - 1:c, :] for p in pairs]
    cc = [cum[p] - cum[p][c // 2 - 1:c // 2, :] for p in pairs]
    e_neg = [jnp.exp(-cc[p]) for p in pairs]
    at = [-kk[p] * jnp.exp(cc[p] - lw[p]) for p in pairs]
    rt = [r[p] * jnp.exp(cc[p]) for p in pairs]
    lhs = [jnp.concatenate([jnp.where(head_a, at[p], 0.0), jnp.where(head_a, 0.0, at[p]),
                            jnp.where(head_a, rt[p], 0.0), jnp.where(head_a, 0.0, rt[p])],
                           axis=0).astype(BF16) for p in pairs]
    rhs = [jnp.concatenate([b[p] * e_neg[p], k[p] * e_neg[p]], axis=0).astype(BF16) for p in pairs]
    gram = [nt(lhs[p], rhs[p]) for p in pairs]
    s0 = [s_prev[p].astype(BF16) for p in pairs]
    v16 = [v[p].astype(BF16) for p in pairs]
    state_u = [nt((-kk[p] * jnp.exp(cum[p] - lw[p])).astype(BF16), s0[p]) for p in pairs]
    state_y = [nt((r[p] * jnp.exp(cum[p])).astype(BF16), s0[p]) for p in pairs]

    pw = [jnp.where(strict, gram[p][h * c:(h + 1) * c, :c], 0.0) for p, h in units]
    inv = [eye + x for x in pw]
    for _ in range(6):
        pw16 = [x.astype(BF16) for x in pw]
        pw = [mm(x, x) for x in pw16]
        inv = [i + _bdot(i, x) for i, x in zip(inv, pw)]
    a_ak = [jnp.where(strict, gram[p][h * c:(h + 1) * c, c:], 0.0).astype(BF16) for p, h in units]
    rhs_u = [state_u[p] + mm(a_ak[2 * p + h], v16[p]) for p, h in units]
    us = [_bdot(i, x) for i, x in zip(inv, rhs_u)]
    u16 = [jnp.where(head_a, us[2 * p], us[2 * p + 1]).astype(BF16) for p in pairs]

    a_rb = [jnp.where(lower, gram[p][(2 + h) * c:(3 + h) * c, :c], 0.0).astype(BF16)
            for p, h in units]
    a_rk = [jnp.where(lower, gram[p][(2 + h) * c:(3 + h) * c, c:], 0.0).astype(BF16)
            for p, h in units]
    yh = [mm(a_rb[2 * p + h], u16[p]) + mm(a_rk[2 * p + h], v16[p]) for p, h in units]
    ys = [state_y[p] + jnp.where(head_a, yh[2 * p], yh[2 * p + 1]) for p in pairs]

    e_end = [jnp.exp(cend[p] - cum[p]) for p in pairs]
    upd = [lax.dot_general(jnp.concatenate([u16[p], v16[p]], axis=0),
                           jnp.concatenate([b[p] * e_end[p], k[p] * e_end[p]], axis=0).astype(BF16),
                           TN_DIMS, preferred_element_type=F32) for p in pairs]
    srow = lax.broadcasted_iota(jnp.int32, (LANES, LANES), 0) // HEAD_DIM
    scol = lax.broadcasted_iota(jnp.int32, (LANES, LANES), 1) // HEAD_DIM
    same_head = srow == scol
    s_new = [s_prev[p] * jnp.exp(cend[p]) + jnp.where(same_head, upd[p], 0.0) for p in pairs]
    return ys, s_new


def _wkv_kernel(r_ref, k_ref, v_ref, kk_ref, b_ref, lw_ref, y_ref, s_ref):
    @pl.when(pl.program_id(2) == 0)
    def _():
        s_ref[...] = jnp.zeros_like(s_ref)

    n_pairs = s_ref.shape[0]
    lanes = [slice(p * LANES, (p + 1) * LANES) for p in range(n_pairs)]
    load = lambda ref: [ref[:, sl].astype(F32) for sl in lanes]
    ys, s_new = _wkv_chunk(load(r_ref), load(k_ref), load(v_ref), load(kk_ref), load(b_ref),
                           load(lw_ref), [s_ref[p] for p in range(n_pairs)])
    for p in range(n_pairs):
        y_ref[:, lanes[p]] = ys[p]
        s_ref[p] = s_new[p]


def _wkv_scan(r, k, v, kk, b, lw, batch, seq, *, pairs=4):
    m, w = r.shape
    c = WKV_CHUNK
    nchunk = seq // c
    spec = pl.BlockSpec((c, pairs * LANES), lambda bi, p, ci: (bi * nchunk + ci, p))
    return pl.pallas_call(
        _wkv_kernel,
        grid=(batch, w // (pairs * LANES), nchunk),
        in_specs=[spec] * 6,
        out_specs=spec,
        out_shape=jax.ShapeDtypeStruct((m, w), F32),
        scratch_shapes=[pltpu.VMEM((pairs, LANES, LANES), F32)],
        compiler_params=_params("parallel", "parallel", "arbitrary"),
        name="wkv_scan",
    )(r, k, v, kk, b, lw)


def _mix_post_kernel(y_ref, g_ref, bonus_ref, lnw_ref, lnb_ref, bg_ref, cg_ref, hv_ref, hvt_ref,
                     cgp_ref, hvp_ref, hvpt_ref, cw_ref, o_ref, *, tiles_per_seq, lane_pad):
    first = (pl.program_id(0) % tiles_per_seq) == 0
    w = y_ref.shape[1]
    y = y_ref[...]
    mean = _head_sum(y) * (1.0 / HEAD_DIM)
    d = y - mean
    var = _head_sum(d * d) * (1.0 / HEAD_DIM)
    yn = d * lax.rsqrt(var + GN_EPS) * lnw_ref[...] + lnb_ref[...]
    o_ref[:, :w] = ((yn + bonus_ref[...].astype(F32)) * g_ref[...].astype(F32)).astype(BF16)
    z = cg_ref[...] * jnp.concatenate([hv_ref[...], hvt_ref[...]], axis=1)
    zp = cgp_ref[...] * jnp.concatenate([hvp_ref[...], hvpt_ref[...]], axis=1)
    cw = cw_ref[...]
    zc = (cw[0:1, :] * _shift_rows(z, zp, 2, first) + cw[1:2, :] * _shift_rows(z, zp, 1, first)
          + cw[2:3, :] * z)
    o_ref[:, w:] = (bg_ref[...] * zc)[:, lane_pad:lane_pad + w].astype(BF16)


def _mix_post(y, g, bonus, ln_w, ln_b, proj, conv_w, conv_start, seq, *, tt=128):
    m, w = y.shape
    lane_pad = conv_start % LANES
    assert lane_pad and conv_start + 3 * w == proj.shape[1], "windows assume a ragged conv tail"
    win = w + LANES
    starts = [conv_start - lane_pad + j * w for j in range(3)]
    tail_blk = (starts[2] + w) // LANES
    cw = jnp.pad(conv_w.T, ((0, 8 - CONV_K), (lane_pad, win - w - lane_pad)))
    row = lambda i: (i, 0)
    const = lambda i: (0, 0)
    prev_row = lambda i: jnp.maximum(i * tt - 8, 0)
    cur = lambda s, width: pl.BlockSpec((pl.Element(tt), pl.Element(width)),
                                        lambda i: (i * tt, s))
    prev = lambda s, width: pl.BlockSpec((pl.Element(8), pl.Element(width)),
                                         lambda i: (pl.multiple_of(prev_row(i), 8), s))
    return pl.pallas_call(
        functools.partial(_mix_post_kernel, tiles_per_seq=seq // tt, lane_pad=lane_pad),
        grid=(m // tt,),
        in_specs=[pl.BlockSpec((tt, w), row)] * 3
        + [pl.BlockSpec((1, w), const)] * 2
        + [cur(starts[0], win), cur(starts[1], win), cur(starts[2], w),
           pl.BlockSpec((tt, LANES), lambda i: (i, tail_blk)),
           prev(starts[1], win), prev(starts[2], w),
           pl.BlockSpec((8, LANES), lambda i: (prev_row(i) // 8, tail_blk)),
           pl.BlockSpec((8, win), const)],
        out_specs=pl.BlockSpec((tt, 2 * w), row),
        out_shape=jax.ShapeDtypeStruct((m, 2 * w), BF16),
        compiler_params=_params("parallel"),
        name="mix_post",
    )(y, g, bonus, ln_w, ln_b, proj, proj, proj, proj, proj, proj, proj, cw)


def _matmul_residual_kernel(a_ref, w_ref, res_ref, o_ref):
    o_ref[...] = res_ref[...] + jnp.dot(a_ref[...], w_ref[...].astype(BF16),
                                        preferred_element_type=F32)


def _matmul_residual(a_bf16, w_stack, layer, res, *, tm=1024, tn=512):
    m, kd = a_bf16.shape
    n = w_stack.shape[2]
    return pl.pallas_call(
        _matmul_residual_kernel,
        grid=(m // tm, n // tn),
        in_specs=[pl.BlockSpec((tm, kd), lambda i, j: (i, 0)),
                  pl.BlockSpec((None, kd, tn), lambda i, j: (layer, 0, j)),
                  pl.BlockSpec((tm, tn), lambda i, j: (i, j))],
        out_specs=pl.BlockSpec((tm, tn), lambda i, j: (i, j)),
        out_shape=jax.ShapeDtypeStruct((m, n), F32),
        compiler_params=_params("parallel", "arbitrary"),
        name="matmul_residual",
    )(a_bf16, w_stack, res)


def _top_values(s, count):
    tops = []
    for _ in range(count):
        mx = jnp.max(s, axis=0, keepdims=True)
        tops.append(mx)
        s = jnp.where(s >= mx, -jnp.inf, s)
    return tops


def _peer_select_kernel(q_ref, keys_ref, th_o, e1_o, s2_o, st_o):
    q = q_ref[...]
    half = q.shape[1] // 2
    sc = []
    for c in range(2):
        qc = q[:, c * half:(c + 1) * half]
        sc.append(lax.dot_general(keys_ref[0, c].astype(BF16), qc.astype(BF16), NT_DIMS,
                                  preferred_element_type=F32))
    top1 = _top_values(sc[0], PEER_TOPK)
    top2 = _top_values(sc[1], PEER_TOPK)
    t1 = jnp.concatenate(top1, axis=0)
    t2 = jnp.concatenate(top2, axis=0)
    row8 = lax.broadcasted_iota(jnp.int32, (8, t2.shape[1]), 0)
    groups = [top1[0] + t2]
    for m_ in range(1, 8):
        n_keep = PEER_TOPK // (m_ + 1)
        g8 = top1[m_] + t2[:8]
        groups.append(g8 if n_keep >= 8 else jnp.where(row8 < n_keep, g8, -jnp.inf))
    groups.append(t1[8:] + top2[0])
    cand = jnp.concatenate(groups, axis=0)
    best = _top_values(cand, PEER_TOPK)
    cmax = best[0]
    z = jnp.exp(best[0] - cmax)
    for bv in best[1:]:
        z = z + jnp.exp(bv - cmax)
    tau = best[-1]
    th = jnp.full_like(sc[0], jnp.inf)
    for m_ in range(PEER_TOPK):
        th_m = jnp.min(jnp.where(top1[m_] + t2 >= tau, t2, jnp.inf), axis=0, keepdims=True)
        th = jnp.where(sc[0] == top1[m_], th_m, th)
    th_o[0] = th
    e1_o[0] = jnp.exp(sc[0] - top1[0]) * (1.0 / z)
    s2_o[0] = sc[1]
    st_o[0] = jnp.concatenate([top2[0], jnp.zeros((STAT_ROWS - 1, z.shape[1]), F32)], axis=0)


def _peer_select(q, sub_keys, *, tt=512):
    assert PEER_TOPK == 16, "candidate groups in _peer_select_kernel are laid out for K = 16"
    m = q.shape[0]
    heads, _, nk, half = sub_keys.shape
    big = jax.ShapeDtypeStruct((heads, nk, m), F32)
    bspec = pl.BlockSpec((1, nk, tt), lambda i, h: (h, 0, i))
    return pl.pallas_call(
        _peer_select_kernel,
        grid=(m // tt, heads),
        in_specs=[pl.BlockSpec((tt, 2 * half), lambda i, h: (i, h)),
                  pl.BlockSpec((1, 2, nk, half), lambda i, h: (h, 0, 0, 0))],
        out_specs=[bspec] * 3 + [pl.BlockSpec((1, STAT_ROWS, tt), lambda i, h: (h, 0, i))],
        out_shape=[big] * 3 + [jax.ShapeDtypeStruct((heads, STAT_ROWS, m), F32)],
        compiler_params=_params("parallel", "arbitrary"),
        name="peer_select",
    )(q, sub_keys)


def _peer_mlp_kernel(x_ref, down_ref, up_ref, th_ref, e1_ref, s2_ref, st_ref, o_ref, e2_scr,
                     gate_scr):
    heads = s2_ref.shape[0]
    sub = 2 * N_KEYS

    @pl.when(pl.program_id(1) == 0)
    def _():
        o_ref[...] = jnp.zeros_like(o_ref)
        for h in range(heads):
            e2_scr[h] = jnp.exp(s2_ref[h] - st_ref[h, 0:1, :])

    tm = x_ref.shape[0]
    te = down_ref.shape[0]
    first_row = pl.program_id(1) * (te // N_KEYS)

    def gate_unit(ii, t0):
        tk = slice(t0, t0 + LANES)
        acc = None
        for h in range(heads):
            th = th_ref[h, pl.ds(first_row + ii, 1), :][:, tk]
            e1 = e1_ref[h, pl.ds(first_row + ii, 1), :][:, tk]
            term = jnp.where(s2_ref[h, :, tk] >= th, e1 * e2_scr[h, :, tk], 0.0)
            acc = term if acc is None else acc + term
        gate_scr[tk, ii * N_KEYS:(ii + 1) * N_KEYS] = acc.T

    for ii in range(te // N_KEYS):
        for t0 in range(0, tm, LANES):
            gate_unit(ii, t0)
    x = x_ref[...]
    hid = []
    for j in range(te // sub):
        cols = slice(j * sub, (j + 1) * sub)
        act = lax.dot_general(x, down_ref[cols, :], NT_DIMS, preferred_element_type=F32)
        gelu = 0.5 * act * (1.0 + lax.erf(act * (2.0 ** -0.5)))
        hid.append((gelu * gate_scr[:, cols]).astype(BF16))
    o_ref[...] += jnp.dot(jnp.concatenate(hid, axis=1), up_ref[...], preferred_element_type=F32)


def _peer_mlp(xn_bf16, down_bf16, up_bf16, th, e1, s2, stats, *, tm=512, te=512):
    m, d = xn_bf16.shape
    ne = down_bf16.shape[0]
    heads, nk, _ = s2.shape
    big = pl.BlockSpec((heads, nk, tm), lambda i, e: (0, 0, i))
    return pl.pallas_call(
        _peer_mlp_kernel,
        grid=(m // tm, ne // te),
        in_specs=[pl.BlockSpec((tm, d), lambda i, e: (i, 0)),
                  pl.BlockSpec((te, d), lambda i, e: (e, 0)),
                  pl.BlockSpec((te, d), lambda i, e: (e, 0)),
                  big, big, big,
                  pl.BlockSpec((heads, STAT_ROWS, tm), lambda i, e: (0, 0, i))],
        out_specs=pl.BlockSpec((tm, d), lambda i, e: (i, 0)),
        out_shape=jax.ShapeDtypeStruct((m, d), F32),
        scratch_shapes=[pltpu.VMEM((heads, nk, tm), F32), pltpu.VMEM((tm, te), F32)],
        compiler_params=pltpu.CompilerParams(dimension_semantics=("parallel", "arbitrary"),
                                             vmem_limit_bytes=PEER_VMEM_LIMIT),
        name="peer_mlp",
    )(xn_bf16, down_bf16, up_bf16, th, e1, s2, stats)


def _final_norm_kernel(h_ref, p_ref, nw_ref, o_ref):
    x = h_ref[...] + p_ref[...]
    y = x * lax.rsqrt(jnp.mean(x * x, axis=-1, keepdims=True) + NORM_EPS)
    o_ref[...] = y * nw_ref[...]


def _final_norm(h, p, nw, *, tt=256):
    m, d = h.shape
    row = lambda i: (i, 0)
    return pl.pallas_call(
        _final_norm_kernel,
        grid=(m // tt,),
        in_specs=[pl.BlockSpec((tt, d), row), pl.BlockSpec((tt, d), row),
                  pl.BlockSpec((1, d), lambda i: (0, 0))],
        out_specs=pl.BlockSpec((tt, d), row),
        out_shape=jax.ShapeDtypeStruct((m, d), F32),
        compiler_params=_params("parallel"),
        name="final_norm",
    )(h, p, nw.reshape(1, d))


def _pad_rows(mat, start, total):
    return jnp.pad(mat, ((start, total - start - mat.shape[0]), (0, 0)))


def kernel(x, norm1_w, w_in, mu_shift, w0, w2, a0, a2, g2, k_k, k_a, r_k, ln_x_w, ln_x_b,
           conv_w, w_out, norm2_w, w_q, sub_keys, expert_down, expert_up, norm_f_w):
    batch, seq, d = x.shape
    depth = w_in.shape[0]
    w = w0.shape[1]
    rwkv_in = 3 * w + DECAY_LORA + ICLR_LORA + GATE_LORA
    lora_w = rwkv_in - 3 * w
    m = batch * seq
    h = x.reshape(m, d)
    row = lambda t: t.reshape(1, -1)
    for l in range(depth):
        proj = _norm_matmul(h, norm1_w[l], w_in, l, emit_xn=False, tm=1024)
        mu = mu_shift[l]
        mu_lora = jnp.pad(mu[3 * w:], (0, LORA_PAD - lora_w))
        w2p = _pad_rows(w2[l], 0, LORA_PAD).astype(BF16)
        a2p = _pad_rows(a2[l], DECAY_LORA, LORA_PAD).astype(BF16)
        g2p = _pad_rows(g2[l], DECAY_LORA + ICLR_LORA, LORA_PAD).astype(BF16)
        r, k2, v, kk, bvec, lw, g, bonus = _rwkv_prep(
            proj, seq, w, row(mu[:3 * w]), row(mu_lora), row(w0[l]), row(a0[l]), row(k_k[l]),
            row(k_a[l]), row(r_k[l]), w2p, a2p, g2p)
        y = _wkv_scan(r, k2, v, kk, bvec, lw, batch, seq)
        mixed = _mix_post(y, g, bonus, row(ln_x_w[l]), row(ln_x_b[l]), proj, conv_w[l], rwkv_in,
                          seq)
        h = _matmul_residual(mixed, w_out, l, h)
        q, xn = _norm_matmul(h, norm2_w[l], w_q, l, emit_xn=True, tm=512)
        th, e1, s2, stats = _peer_select(q, sub_keys[l])
        peer = _peer_mlp(xn, expert_down[l].astype(BF16), expert_up[l].astype(BF16),
                         th, e1, s2, stats)
        if l + 1 < depth:
            h = h + peer
    return _final_norm(h, peer, norm_f_w).reshape(batch, seq, d)
```

```python
import functools

import jax
import jax.numpy as jnp
from jax import lax
from jax.experimental import pallas as pl
from jax.experimental.pallas import tpu as pltpu

F32 = jnp.float32
BF16 = jnp.bfloat16

HEAD_DIM = 64
LANES = 128
NORM_EPS = 1e-6
GN_EPS = 64e-5
DECAY_LORA = 96
ICLR_LORA = 96
GATE_LORA = 256
LORA_PAD = 512
CONV_K = 3
PEER_HEADS = 8
N_KEYS = 128
PEER_TOPK = 16
WKV_CHUNK = 128
STAT_ROWS = 8
NORM_ROWS = 128
VMEM_LIMIT = 56 * 1024 * 1024
PEER_VMEM_LIMIT = 60 * 1024 * 1024

NT_DIMS = (((1,), (1,)), ((), ()))
TN_DIMS = (((0,), (0,)), ((), ()))


def _params(*sem):
    return pltpu.CompilerParams(dimension_semantics=sem, vmem_limit_bytes=VMEM_LIMIT)


def _bdot(a, b):
    return jnp.dot(a.astype(BF16), b.astype(BF16), preferred_element_type=F32)


def _split3(x):
    hi = x.astype(BF16)
    r1 = x - hi.astype(F32)
    mid = r1.astype(BF16)
    lo = (r1 - mid.astype(F32)).astype(BF16)
    return hi, mid, lo


def _dot_exact_lhs(a_bf16, x):
    hi, mid, lo = _split3(x)
    d = lambda t: jnp.dot(a_bf16, t, preferred_element_type=F32)
    return d(hi) + d(mid) + d(lo)


def _dot_exact_rhs(x, b_bf16):
    hi, mid, lo = _split3(x)
    d = lambda t: jnp.dot(t, b_bf16, preferred_element_type=F32)
    return d(hi) + d(mid) + d(lo)


def _head_ones():
    r = lax.broadcasted_iota(jnp.int32, (LANES, LANES), 0) // HEAD_DIM
    c = lax.broadcasted_iota(jnp.int32, (LANES, LANES), 1) // HEAD_DIM
    return (r == c).astype(BF16)


def _head_sum(x):
    ones = _head_ones()
    parts = [_dot_exact_rhs(x[:, j:j + LANES], ones) for j in range(0, x.shape[1], LANES)]
    return jnp.concatenate(parts, axis=1)


def _norm_matmul_kernel(x_ref, nw_ref, w_ref, o_ref, *rest, emit_xn):
    xn_scr = rest[-1]

    @pl.when(pl.program_id(1) == 0)
    def _():
        for r0 in range(0, x_ref.shape[0], NORM_ROWS):
            x = x_ref[r0:r0 + NORM_ROWS, :]
            y = x * lax.rsqrt(jnp.mean(x * x, axis=-1, keepdims=True) + NORM_EPS)
            xn_scr[r0:r0 + NORM_ROWS, :] = (y * nw_ref[...]).astype(BF16)

    o_ref[...] = jnp.dot(xn_scr[...], w_ref[...].astype(BF16), preferred_element_type=F32)
    if emit_xn:
        @pl.when(pl.program_id(1) == 0)
        def _():
            rest[0][...] = xn_scr[...]


def _norm_matmul(x, nw, w_stack, layer, *, emit_xn, tm, tn=512):
    m, d = x.shape
    n = w_stack.shape[2]
    out_shape = [jax.ShapeDtypeStruct((m, n), F32)]
    out_specs = [pl.BlockSpec((tm, tn), lambda i, j: (i, j))]
    if emit_xn:
        out_shape.append(jax.ShapeDtypeStruct((m, d), BF16))
        out_specs.append(pl.BlockSpec((tm, d), lambda i, j: (i, 0)))
    res = pl.pallas_call(
        functools.partial(_norm_matmul_kernel, emit_xn=emit_xn),
        grid=(m // tm, pl.cdiv(n, tn)),
        in_specs=[pl.BlockSpec((tm, d), lambda i, j: (i, 0), pipeline_mode=pl.Buffered(1)),
                  pl.BlockSpec((1, d), lambda i, j: (0, 0)),
                  pl.BlockSpec((None, d, tn), lambda i, j: (layer, 0, j))],
        out_specs=out_specs,
        out_shape=out_shape,
        scratch_shapes=[pltpu.VMEM((tm, d), BF16)],
        compiler_params=_params("parallel", "arbitrary"),
        name="norm_matmul",
    )(x, nw.reshape(1, d), w_stack)
    return res if emit_xn else res[0]


def _shift_rows(x, prev8, n, first):
    rows = lax.broadcasted_iota(jnp.int32, x.shape, 0)
    out = pltpu.roll(x, n, 0)
    for j in range(n):
        fill = jnp.where(first, 0.0, prev8[8 - n + j:8 - n + j + 1, :])
        out = jnp.where(rows == j, fill, out)
    return out


def _rwkv_prep_kernel(p_ref, pl_ref, pp_ref, ppl_ref, mu_ref, mul_ref, w0_ref, a0_ref,
                      kk_ref, ka_ref, rk_ref, w2_ref, a2_ref, g2_ref,
                      r_o, k_o, v_o, kk_o, b_o, lw_o, g_o, bonus_o, *, tiles_per_seq):
    first = (pl.program_id(0) % tiles_per_seq) == 0
    w = r_o.shape[1]
    p = p_ref[...]
    ps = p + (_shift_rows(p, pp_ref[...], 1, first) - p) * mu_ref[...]
    q = pl_ref[...]
    qs = q + (_shift_rows(q, ppl_ref[...], 1, first) - q) * mul_ref[...]
    r = ps[:, :w]
    k = ps[:, w:2 * w]
    v = ps[:, 2 * w:]
    z = w0_ref[...] + _bdot(jnp.tanh(qs), w2_ref[...])
    w_log = jnp.minimum(z, 0.0) - jnp.log(1.0 + jnp.exp(-jnp.abs(z))) - 0.5
    a = jax.nn.sigmoid(a0_ref[...] + _bdot(qs, a2_ref[...]))
    g = _bdot(jax.nn.sigmoid(qs), g2_ref[...])
    kk = k * kk_ref[...]
    kk = kk * lax.rsqrt(jnp.maximum(_head_sum(kk * kk), 1e-24))
    k2 = k * (1.0 + (a - 1.0) * ka_ref[...])
    r_o[...] = r.astype(BF16)
    k_o[...] = k2.astype(BF16)
    v_o[...] = v.astype(BF16)
    kk_o[...] = kk.astype(BF16)
    b_o[...] = (kk * a).astype(BF16)
    lw_o[...] = -jnp.exp(w_log)
    g_o[...] = g.astype(BF16)
    bonus_o[...] = (_head_sum(r * k2 * rk_ref[...]) * v).astype(BF16)


def _rwkv_prep(proj, seq, w, mu_rkv, mu_lora, w0, a0, k_k, k_a, r_k, w2p, a2p, g2p, *, tt=128):
    m = proj.shape[0]
    lora_blk = (3 * w) // LORA_PAD
    row = lambda i: (i, 0)
    prev = lambda i: (jnp.maximum(i * (tt // 8) - 1, 0), 0)
    prev_l = lambda i: (jnp.maximum(i * (tt // 8) - 1, 0), lora_blk)
    const = lambda i: (0, 0)
    vec = pl.BlockSpec((1, w), const)
    out = jax.ShapeDtypeStruct((m, w), F32)
    out16 = jax.ShapeDtypeStruct((m, w), BF16)
    return pl.pallas_call(
        functools.partial(_rwkv_prep_kernel, tiles_per_seq=seq // tt),
        grid=(m // tt,),
        in_specs=[pl.BlockSpec((tt, 3 * w), row),
                  pl.BlockSpec((tt, LORA_PAD), lambda i: (i, lora_blk)),
                  pl.BlockSpec((8, 3 * w), prev),
                  pl.BlockSpec((8, LORA_PAD), prev_l),
                  pl.BlockSpec((1, 3 * w), const),
                  pl.BlockSpec((1, LORA_PAD), const),
                  vec, vec, vec, vec, vec,
                  pl.BlockSpec((LORA_PAD, w), const),
                  pl.BlockSpec((LORA_PAD, w), const),
                  pl.BlockSpec((LORA_PAD, w), const)],
        out_specs=[pl.BlockSpec((tt, w), row)] * 8,
        out_shape=[out16] * 5 + [out, out16, out16],
        compiler_params=_params("parallel"),
        name="rwkv_prep",
    )(proj, proj, proj, proj, mu_rkv, mu_lora, w0, a0, k_k, k_a, r_k, w2p, a2p, g2p)


def _wkv_chunk(r, k, v, kk, b, lw, s_prev):
    c = WKV_CHUNK
    pairs = range(len(r))
    units = [(p, h) for p in pairs for h in range(2)]
    row = lax.broadcasted_iota(jnp.int32, (c, c), 0)
    col = lax.broadcasted_iota(jnp.int32, (c, c), 1)
    lower = row >= col
    strict = row > col
    eye = (row == col).astype(F32)
    low16 = lower.astype(BF16)
    head_a = lax.broadcasted_iota(jnp.int32, (c, LANES), 1) < HEAD_DIM
    nt = lambda x, y: lax.dot_general(x, y, NT_DIMS, preferred_element_type=F32)
    mm = lambda x, y: jnp.dot(x, y, preferred_element_type=F32)

    cum = [_dot_exact_lhs(low16, lw[p]) for p in pairs]
    cend = [cum[p][c - 1:c, :] for p in pairs]
    cc = [cum[p] - cum[p][c // 2 - 1:c // 2, :] for p in pairs]
    e_neg = [jnp.exp(-cc[p]) for p in pairs]
    at = [-kk[p] * jnp.exp(cc[p] - lw[p]) for p in pairs]
    rt = [r[p] * jnp.exp(cc[p]) for p in pairs]
    lhs = [jnp.concatenate([jnp.where(head_a, at[p], 0.0), jnp.where(head_a, 0.0, at[p]),
                            jnp.where(head_a, rt[p], 0.0), jnp.where(head_a, 0.0, rt[p])],
                           axis=0).astype(BF16) for p in pairs]
    rhs = [jnp.concatenate([b[p] * e_neg[p], k[p] * e_neg[p]], axis=0).astype(BF16) for p in pairs]
    gram = [nt(lhs[p], rhs[p]) for p in pairs]
    s0 = [s_prev[p].astype(BF16) for p in pairs]
    v16 = [v[p].astype(BF16) for p in pairs]
    state_u = [nt((-kk[p] * jnp.exp(cum[p] - lw[p])).astype(BF16), s0[p]) for p in pairs]
    state_y = [nt((r[p] * jnp.exp(cum[p])).astype(BF16), s0[p]) for p in pairs]

    pw = [jnp.where(strict, gram[p][h * c:(h + 1) * c, :c], 0.0) for p, h in units]
    inv = [eye + x for x in pw]
    for _ in range(6):
        pw16 = [x.astype(BF16) for x in pw]
        pw = [mm(x, x) for x in pw16]
        inv = [i + _bdot(i, x) for i, x in zip(inv, pw)]
    a_ak = [jnp.where(strict, gram[p][h * c:(h + 1) * c, c:], 0.0).astype(BF16) for p, h in units]
    rhs_u = [state_u[p] + mm(a_ak[2 * p + h], v16[p]) for p, h in units]
    us = [_bdot(i, x) for i, x in zip(inv, rhs_u)]
    u16 = [jnp.where(head_a, us[2 * p], us[2 * p + 1]).astype(BF16) for p in pairs]

    a_rb = [jnp.where(lower, gram[p][(2 + h) * c:(3 + h) * c, :c], 0.0).astype(BF16)
            for p, h in units]
    a_rk = [jnp.where(lower, gram[p][(2 + h) * c:(3 + h) * c, c:], 0.0).astype(BF16)
            for p, h in units]
    yh = [mm(a_rb[2 * p + h], u16[p]) + mm(a_rk[2 * p + h], v16[p]) for p, h in units]
    ys = [state_y[p] + jnp.where(head_a, yh[2 * p], yh[2 * p + 1]) for p in pairs]

    e_end = [jnp.exp(cend[p] - cum[p]) for p in pairs]
    upd = [lax.dot_general(jnp.concatenate([u16[p], v16[p]], axis=0),
                           jnp.concatenate([b[p] * e_end[p], k[p] * e_end[p]], axis=0).astype(BF16),
                           TN_DIMS, preferred_element_type=F32) for p in pairs]
    srow = lax.broadcasted_iota(jnp.int32, (LANES, LANES), 0) // HEAD_DIM
    scol = lax.broadcasted_iota(jnp.int32, (LANES, LANES), 1) // HEAD_DIM
    same_head = srow == scol
    s_new = [s_prev[p] * jnp.exp(cend[p]) + jnp.where(same_head, upd[p], 0.0) for p in pairs]
    return ys, s_new


def _wkv_kernel(r_ref, k_ref, v_ref, kk_ref, b_ref, lw_ref, y_ref, s_ref):
    @pl.when(pl.program_id(2) == 0)
    def _():
        s_ref[...] = jnp.zeros_like(s_ref)

    n_pairs = s_ref.shape[0]
    lanes = [slice(p * LANES, (p + 1) * LANES) for p in range(n_pairs)]
    load = lambda ref: [ref[:, sl].astype(F32) for sl in lanes]
    ys, s_new = _wkv_chunk(load(r_ref), load(k_ref), load(v_ref), load(kk_ref), load(b_ref),
                           load(lw_ref), [s_ref[p] for p in range(n_pairs)])
    for p in range(n_pairs):
        y_ref[:, lanes[p]] = ys[p]
        s_ref[p] = s_new[p]


def _wkv_scan(r, k, v, kk, b, lw, batch, seq, *, pairs=8):
    m, w = r.shape
    c = WKV_CHUNK
    nchunk = seq // c
    spec = pl.BlockSpec((c, pairs * LANES), lambda bi, p, ci: (bi * nchunk + ci, p))
    return pl.pallas_call(
        _wkv_kernel,
        grid=(batch, w // (pairs * LANES), nchunk),
        in_specs=[spec] * 6,
        out_specs=spec,
        out_shape=jax.ShapeDtypeStruct((m, w), F32),
        scratch_shapes=[pltpu.VMEM((pairs, LANES, LANES), F32)],
        compiler_params=_params("parallel", "parallel", "arbitrary"),
        name="wkv_scan",
    )(r, k, v, kk, b, lw)


def _mix_post_kernel(y_ref, g_ref, bonus_ref, lnw_ref, lnb_ref, bg_ref, cg_ref, hv_ref,
                     cgp_ref, hvp_ref, cw_ref, o_ref, *, tiles_per_seq, lane_pad):
    first = (pl.program_id(0) % tiles_per_seq) == 0
    w = y_ref.shape[1]
    y = y_ref[...]
    mean = _head_sum(y) * (1.0 / HEAD_DIM)
    d = y - mean
    var = _head_sum(d * d) * (1.0 / HEAD_DIM)
    yn = d * lax.rsqrt(var + GN_EPS) * lnw_ref[...] + lnb_ref[...]
    o_ref[:, :w] = ((yn + bonus_ref[...].astype(F32)) * g_ref[...].astype(F32)).astype(BF16)
    z = cg_ref[...] * hv_ref[...]
    zp = cgp_ref[...] * hvp_ref[...]
    cw = cw_ref[...]
    zc = (cw[0:1, :] * _shift_rows(z, zp, 2, first) + cw[1:2, :] * _shift_rows(z, zp, 1, first)
          + cw[2:3, :] * z)
    o_ref[:, w:] = (bg_ref[...] * zc)[:, lane_pad:lane_pad + w].astype(BF16)


def _mix_post(y, g, bonus, ln_w, ln_b, proj, conv_w, conv_start, seq, *, tt=128):
    m, w = y.shape
    lane_pad = conv_start % LANES
    win = w + (LANES if lane_pad else 0)
    starts = [conv_start - lane_pad + j * w for j in range(3)]
    assert starts[2] + win <= proj.shape[1], "proj must be lane-padded so every window is in bounds"
    cw = jnp.pad(conv_w.T, ((0, 8 - CONV_K), (lane_pad, win - w - lane_pad)))
    row = lambda i: (i, 0)
    const = lambda i: (0, 0)
    cur = lambda s: pl.BlockSpec((pl.Element(tt), pl.Element(win)), lambda i: (i * tt, s))
    prev = lambda s: pl.BlockSpec((pl.Element(8), pl.Element(win)),
                                  lambda i: (pl.multiple_of(jnp.maximum(i * tt - 8, 0), 8), s))
    return pl.pallas_call(
        functools.partial(_mix_post_kernel, tiles_per_seq=seq // tt, lane_pad=lane_pad),
        grid=(m // tt,),
        in_specs=[pl.BlockSpec((tt, w), row)] * 3
        + [pl.BlockSpec((1, w), const)] * 2
        + [cur(s) for s in starts]
        + [prev(starts[1]), prev(starts[2]), pl.BlockSpec((8, win), const)],
        out_specs=pl.BlockSpec((tt, 2 * w), row),
        out_shape=jax.ShapeDtypeStruct((m, 2 * w), BF16),
        compiler_params=_params("parallel"),
        name="mix_post",
    )(y, g, bonus, ln_w, ln_b, proj, proj, proj, proj, proj, cw)


def _matmul_residual_kernel(a_ref, w_ref, res_ref, o_ref):
    o_ref[...] = res_ref[...] + jnp.dot(a_ref[...], w_ref[...].astype(BF16),
                                        preferred_element_type=F32)


def _matmul_residual(a_bf16, w_stack, layer, res, *, tm=1024, tn=512):
    m, kd = a_bf16.shape
    n = w_stack.shape[2]
    return pl.pallas_call(
        _matmul_residual_kernel,
        grid=(m // tm, n // tn),
        in_specs=[pl.BlockSpec((tm, kd), lambda i, j: (i, 0)),
                  pl.BlockSpec((None, kd, tn), lambda i, j: (layer, 0, j)),
                  pl.BlockSpec((tm, tn), lambda i, j: (i, j))],
        out_specs=pl.BlockSpec((tm, tn), lambda i, j: (i, j)),
        out_shape=jax.ShapeDtypeStruct((m, n), F32),
        compiler_params=_params("parallel", "arbitrary"),
        name="matmul_residual",
    )(a_bf16, w_stack, res)


def _top_values(s, count):
    tops = []
    for _ in range(count):
        mx = jnp.max(s, axis=0, keepdims=True)
        tops.append(mx)
        s = jnp.where(s >= mx, -jnp.inf, s)
    return tops


def _peer_select_kernel(q_ref, keys_ref, th_o, e1_o, s2_o, st_o):
    q = q_ref[...]
    half = q.shape[1] // 2
    sc = []
    for c in range(2):
        qc = q[:, c * half:(c + 1) * half]
        sc.append(lax.dot_general(keys_ref[0, c].astype(BF16), qc.astype(BF16), NT_DIMS,
                                  preferred_element_type=F32))
    top1 = _top_values(sc[0], PEER_TOPK)
    top2 = _top_values(sc[1], PEER_TOPK)
    t1 = jnp.concatenate(top1, axis=0)
    t2 = jnp.concatenate(top2, axis=0)
    row8 = lax.broadcasted_iota(jnp.int32, (8, t2.shape[1]), 0)
    groups = [top1[0] + t2]
    for m_ in range(1, 8):
        n_keep = PEER_TOPK // (m_ + 1)
        g8 = top1[m_] + t2[:8]
        groups.append(g8 if n_keep >= 8 else jnp.where(row8 < n_keep, g8, -jnp.inf))
    groups.append(t1[8:] + top2[0])
    cand = jnp.concatenate(groups, axis=0)
    best = _top_values(cand, PEER_TOPK)
    cmax = best[0]
    z = jnp.exp(best[0] - cmax)
    for bv in best[1:]:
        z = z + jnp.exp(bv - cmax)
    tau = best[-1]
    th = jnp.full_like(sc[0], jnp.inf)
    for m_ in range(PEER_TOPK):
        th_m = jnp.min(jnp.where(top1[m_] + t2 >= tau, t2, jnp.inf), axis=0, keepdims=True)
        th = jnp.where(sc[0] == top1[m_], th_m, th)
    th_o[0] = th
    e1_o[0] = jnp.exp(sc[0] - top1[0]) * (1.0 / z)
    s2_o[0] = sc[1]
    st_o[0] = jnp.concatenate([top2[0], jnp.zeros((STAT_ROWS - 1, z.shape[1]), F32)], axis=0)


def _peer_select(q, sub_keys, *, tt=512):
    assert PEER_TOPK == 16, "candidate groups in _peer_select_kernel are laid out for K = 16"
    m = q.shape[0]
    heads, _, nk, half = sub_keys.shape
    big = jax.ShapeDtypeStruct((heads, nk, m), F32)
    bspec = pl.BlockSpec((1, nk, tt), lambda i, h: (h, 0, i))
    return pl.pallas_call(
        _peer_select_kernel,
        grid=(m // tt, heads),
        in_specs=[pl.BlockSpec((tt, 2 * half), lambda i, h: (i, h)),
                  pl.BlockSpec((1, 2, nk, half), lambda i, h: (h, 0, 0, 0))],
        out_specs=[bspec] * 3 + [pl.BlockSpec((1, STAT_ROWS, tt), lambda i, h: (h, 0, i))],
        out_shape=[big] * 3 + [jax.ShapeDtypeStruct((heads, STAT_ROWS, m), F32)],
        compiler_params=_params("parallel", "arbitrary"),
        name="peer_select",
    )(q, sub_keys)


def _peer_mlp_kernel(x_ref, down_ref, up_ref, th_ref, e1_ref, s2_ref, st_ref, o_ref, e2_scr,
                     gate_scr):
    heads = s2_ref.shape[0]
    sub = 2 * N_KEYS

    @pl.when(pl.program_id(1) == 0)
    def _():
        o_ref[...] = jnp.zeros_like(o_ref)
        for h in range(heads):
            e2_scr[h] = jnp.exp(s2_ref[h] - st_ref[h, 0:1, :])

    tm = x_ref.shape[0]
    te = down_ref.shape[0]
    first_row = pl.program_id(1) * (te // N_KEYS)

    def gate_unit(ii, t0):
        tk = slice(t0, t0 + LANES)
        acc = None
        for h in range(heads):
            th = th_ref[h, pl.ds(first_row + ii, 1), :][:, tk]
            e1 = e1_ref[h, pl.ds(first_row + ii, 1), :][:, tk]
            term = jnp.where(s2_ref[h, :, tk] >= th, e1 * e2_scr[h, :, tk], 0.0)
            acc = term if acc is None else acc + term
        gate_scr[tk, ii * N_KEYS:(ii + 1) * N_KEYS] = acc.T

    for ii in range(te // N_KEYS):
        for t0 in range(0, tm, LANES):
            gate_unit(ii, t0)
    x = x_ref[...]
    hid = []
    for j in range(te // sub):
        cols = slice(j * sub, (j + 1) * sub)
        act = lax.dot_general(x, down_ref[cols, :], NT_DIMS, preferred_element_type=F32)
        gelu = 0.5 * act * (1.0 + lax.erf(act * (2.0 ** -0.5)))
        hid.append((gelu * gate_scr[:, cols]).astype(BF16))
    o_ref[...] += jnp.dot(jnp.concatenate(hid, axis=1), up_ref[...], preferred_element_type=F32)


def _peer_mlp(xn_bf16, down_bf16, up_bf16, th, e1, s2, stats, *, tm=512, te=512):
    m, d = xn_bf16.shape
    ne = down_bf16.shape[0]
    heads, nk, _ = s2.shape
    big = pl.BlockSpec((heads, nk, tm), lambda i, e: (0, 0, i))
    return pl.pallas_call(
        _peer_mlp_kernel,
        grid=(m // tm, ne // te),
        in_specs=[pl.BlockSpec((tm, d), lambda i, e: (i, 0)),
                  pl.BlockSpec((te, d), lambda i, e: (e, 0)),
                  pl.BlockSpec((te, d), lambda i, e: (e, 0)),
                  big, big, big,
                  pl.BlockSpec((heads, STAT_ROWS, tm), lambda i, e: (0, 0, i))],
        out_specs=pl.BlockSpec((tm, d), lambda i, e: (i, 0)),
        out_shape=jax.ShapeDtypeStruct((m, d), F32),
        scratch_shapes=[pltpu.VMEM((heads, nk, tm), F32), pltpu.VMEM((tm, te), F32)],
        compiler_params=pltpu.CompilerParams(dimension_semantics=("parallel", "arbitrary"),
                                             vmem_limit_bytes=PEER_VMEM_LIMIT),
        name="peer_mlp",
    )(xn_bf16, down_bf16, up_bf16, th, e1, s2, stats)


def _final_norm_kernel(h_ref, p_ref, nw_ref, o_ref):
    x = h_ref[...] + p_ref[...]
    y = x * lax.rsqrt(jnp.mean(x * x, axis=-1, keepdims=True) + NORM_EPS)
    o_ref[...] = y * nw_ref[...]


def _final_norm(h, p, nw, *, tt=256):
    m, d = h.shape
    row = lambda i: (i, 0)
    return pl.pallas_call(
        _final_norm_kernel,
        grid=(m // tt,),
        in_specs=[pl.BlockSpec((tt, d), row), pl.BlockSpec((tt, d), row),
                  pl.BlockSpec((1, d), lambda i: (0, 0))],
        out_specs=pl.BlockSpec((tt, d), row),
        out_shape=jax.ShapeDtypeStruct((m, d), F32),
        compiler_params=_params("parallel"),
        name="final_norm",
    )(h, p, nw.reshape(1, d))


def _pad_rows(mat, start, total):
    return jnp.pad(mat, ((start, total - start - mat.shape[0]), (0, 0)))


def kernel(x, norm1_w, w_in, mu_shift, w0, w2, a0, a2, g2, k_k, k_a, r_k, ln_x_w, ln_x_b,
           conv_w, w_out, norm2_w, w_q, sub_keys, expert_down, expert_up, norm_f_w):
    batch, seq, d = x.shape
    depth = w_in.shape[0]
    w = w0.shape[1]
    rwkv_in = 3 * w + DECAY_LORA + ICLR_LORA + GATE_LORA
    lora_w = rwkv_in - 3 * w
    m = batch * seq
    h = x.reshape(m, d)
    row = lambda t: t.reshape(1, -1)
    for l in range(depth):
        n_in = w_in.shape[2]
        w_proj = jnp.pad(w_in[l], ((0, 0), (0, -n_in % LANES))).astype(BF16)[None]
        proj = _norm_matmul(h, norm1_w[l], w_proj, 0, emit_xn=False, tm=1024)
        mu = mu_shift[l]
        mu_lora = jnp.pad(mu[3 * w:], (0, LORA_PAD - lora_w))
        w2p = _pad_rows(w2[l], 0, LORA_PAD).astype(BF16)
        a2p = _pad_rows(a2[l], DECAY_LORA, LORA_PAD).astype(BF16)
        g2p = _pad_rows(g2[l], DECAY_LORA + ICLR_LORA, LORA_PAD).astype(BF16)
        r, k2, v, kk, bvec, lw, g, bonus = _rwkv_prep(
            proj, seq, w, row(mu[:3 * w]), row(mu_lora), row(w0[l]), row(a0[l]), row(k_k[l]),
            row(k_a[l]), row(r_k[l]), w2p, a2p, g2p)
        y = _wkv_scan(r, k2, v, kk, bvec, lw, batch, seq)
        mixed = _mix_post(y, g, bonus, row(ln_x_w[l]), row(ln_x_b[l]), proj, conv_w[l], rwkv_in,
                          seq)
        h = _matmul_residual(mixed, w_out, l, h)
        q, xn = _norm_matmul(h, norm2_w[l], w_q, l, emit_xn=True, tm=1024, tn=256)
        th, e1, s2, stats = _peer_select(q, sub_keys[l])
        peer = _peer_mlp(xn, expert_down[l].astype(BF16), expert_up[l].astype(BF16),
                         th, e1, s2, stats)
        if l + 1 < depth:
            h = h + peer
    return _final_norm(h, peer, norm_f_w).reshape(batch, seq, d)
```

```python
import functools

import jax
import jax.numpy as jnp
from jax import lax
from jax.experimental import pallas as pl
from jax.experimental.pallas import tpu as pltpu

F32 = jnp.float32
BF16 = jnp.bfloat16

HEAD_DIM = 64
LANES = 128
NORM_EPS = 1e-6
GN_EPS = 64e-5
DECAY_LORA = 96
ICLR_LORA = 96
GATE_LORA = 256
LORA_PAD = 512
CONV_K = 3
PEER_HEADS = 8
N_KEYS = 128
PEER_TOPK = 16
WKV_CHUNK = 128
STAT_ROWS = 8
NORM_ROWS = 128
VMEM_LIMIT = 56 * 1024 * 1024
PEER_VMEM_LIMIT = 60 * 1024 * 1024

NT_DIMS = (((1,), (1,)), ((), ()))
TN_DIMS = (((0,), (0,)), ((), ()))


def _params(*sem):
    return pltpu.CompilerParams(dimension_semantics=sem, vmem_limit_bytes=VMEM_LIMIT)


def _bdot(a, b):
    return jnp.dot(a.astype(BF16), b.astype(BF16), preferred_element_type=F32)


def _split3(x):
    hi = x.astype(BF16)
    r1 = x - hi.astype(F32)
    mid = r1.astype(BF16)
    lo = (r1 - mid.astype(F32)).astype(BF16)
    return hi, mid, lo


def _dot_exact_lhs(a_bf16, x):
    hi, mid, lo = _split3(x)
    d = lambda t: jnp.dot(a_bf16, t, preferred_element_type=F32)
    return d(hi) + d(mid) + d(lo)


def _dot_exact_rhs(x, b_bf16):
    hi, mid, lo = _split3(x)
    d = lambda t: jnp.dot(t, b_bf16, preferred_element_type=F32)
    return d(hi) + d(mid) + d(lo)


def _head_ones():
    r = lax.broadcasted_iota(jnp.int32, (LANES, LANES), 0) // HEAD_DIM
    c = lax.broadcasted_iota(jnp.int32, (LANES, LANES), 1) // HEAD_DIM
    return (r == c).astype(BF16)


def _head_sum(x):
    ones = _head_ones()
    parts = [_dot_exact_rhs(x[:, j:j + LANES], ones) for j in range(0, x.shape[1], LANES)]
    return jnp.concatenate(parts, axis=1)


def _norm_matmul_kernel(x_ref, nw_ref, w_ref, o_ref, *rest, emit_xn, w_rows_are_outputs, n_valid):
    xn_scr = rest[-1]

    @pl.when(pl.program_id(1) == 0)
    def _():
        for r0 in range(0, x_ref.shape[0], NORM_ROWS):
            x = x_ref[r0:r0 + NORM_ROWS, :]
            y = x * lax.rsqrt(jnp.mean(x * x, axis=-1, keepdims=True) + NORM_EPS)
            xn_scr[r0:r0 + NORM_ROWS, :] = (y * nw_ref[...]).astype(BF16)

    w = w_ref[...].astype(BF16)
    if w_rows_are_outputs:
        out = lax.dot_general(xn_scr[...], w, NT_DIMS, preferred_element_type=F32)
    else:
        out = jnp.dot(xn_scr[...], w, preferred_element_type=F32)
    tn = o_ref.shape[1]
    if n_valid % tn:
        col = pl.program_id(1) * tn + lax.broadcasted_iota(jnp.int32, out.shape, 1)
        out = jnp.where(col < n_valid, out, 0.0)
    o_ref[...] = out
    if emit_xn:
        @pl.when(pl.program_id(1) == 0)
        def _():
            rest[0][...] = xn_scr[...]


def _norm_matmul(x, nw, w_stack, layer, *, emit_xn, w_rows_are_outputs, tm, tn=512):
    m, d = x.shape
    n = w_stack.shape[1] if w_rows_are_outputs else w_stack.shape[2]
    n_blocks = pl.cdiv(n, tn)
    if w_rows_are_outputs:
        w_spec = pl.BlockSpec((None, tn, d), lambda i, j: (layer, j, 0))
    else:
        w_spec = pl.BlockSpec((None, d, tn), lambda i, j: (layer, 0, j))
    out_shape = [jax.ShapeDtypeStruct((m, n_blocks * tn), F32)]
    out_specs = [pl.BlockSpec((tm, tn), lambda i, j: (i, j))]
    if emit_xn:
        out_shape.append(jax.ShapeDtypeStruct((m, d), BF16))
        out_specs.append(pl.BlockSpec((tm, d), lambda i, j: (i, 0)))
    res = pl.pallas_call(
        functools.partial(_norm_matmul_kernel, emit_xn=emit_xn,
                          w_rows_are_outputs=w_rows_are_outputs, n_valid=n),
        grid=(m // tm, n_blocks),
        in_specs=[pl.BlockSpec((tm, d), lambda i, j: (i, 0), pipeline_mode=pl.Buffered(1)),
                  pl.BlockSpec((1, d), lambda i, j: (0, 0)),
                  w_spec],
        out_specs=out_specs,
        out_shape=out_shape,
        scratch_shapes=[pltpu.VMEM((tm, d), BF16)],
        compiler_params=_params("parallel", "arbitrary"),
        name="norm_matmul",
    )(x, nw.reshape(1, d), w_stack)
    return res if emit_xn else res[0]


def _shift_rows(x, prev8, n, first):
    rows = lax.broadcasted_iota(jnp.int32, x.shape, 0)
    out = pltpu.roll(x, n, 0)
    for j in range(n):
        fill = jnp.where(first, 0.0, prev8[8 - n + j:8 - n + j + 1, :])
        out = jnp.where(rows == j, fill, out)
    return out


def _rwkv_prep_kernel(p_ref, pl_ref, pp_ref, ppl_ref, mu_ref, mul_ref, w0_ref, a0_ref,
                      kk_ref, ka_ref, rk_ref, w2_ref, a2_ref, g2_ref,
                      r_o, k_o, v_o, kk_o, b_o, lw_o, g_o, bonus_o, *, tiles_per_seq):
    first = (pl.program_id(0) % tiles_per_seq) == 0
    w = r_o.shape[1]
    p = p_ref[...]
    ps = p + (_shift_rows(p, pp_ref[...], 1, first) - p) * mu_ref[...]
    q = pl_ref[...]
    qs = q + (_shift_rows(q, ppl_ref[...], 1, first) - q) * mul_ref[...]
    r = ps[:, :w]
    k = ps[:, w:2 * w]
    v = ps[:, 2 * w:]
    z = w0_ref[...] + _bdot(jnp.tanh(qs), w2_ref[...])
    w_log = jnp.minimum(z, 0.0) - jnp.log(1.0 + jnp.exp(-jnp.abs(z))) - 0.5
    a = jax.nn.sigmoid(a0_ref[...] + _bdot(qs, a2_ref[...]))
    g = _bdot(jax.nn.sigmoid(qs), g2_ref[...])
    kk = k * kk_ref[...]
    kk = kk * lax.rsqrt(jnp.maximum(_head_sum(kk * kk), 1e-24))
    k2 = k * (1.0 + (a - 1.0) * ka_ref[...])
    r_o[...] = r.astype(BF16)
    k_o[...] = k2.astype(BF16)
    v_o[...] = v.astype(BF16)
    kk_o[...] = kk.astype(BF16)
    b_o[...] = (kk * a).astype(BF16)
    lw_o[...] = -jnp.exp(w_log)
    g_o[...] = g.astype(BF16)
    bonus_o[...] = (_head_sum(r * k2 * rk_ref[...]) * v).astype(BF16)


def _rwkv_prep(proj, seq, w, mu_rkv, mu_lora, w0, a0, k_k, k_a, r_k, w2p, a2p, g2p, *, tt=128):
    m = proj.shape[0]
    lora_blk = (3 * w) // LORA_PAD
    row = lambda i: (i, 0)
    prev = lambda i: (jnp.maximum(i * (tt // 8) - 1, 0), 0)
    prev_l = lambda i: (jnp.maximum(i * (tt // 8) - 1, 0), lora_blk)
    const = lambda i: (0, 0)
    vec = pl.BlockSpec((1, w), const)
    out = jax.ShapeDtypeStruct((m, w), F32)
    out16 = jax.ShapeDtypeStruct((m, w), BF16)
    return pl.pallas_call(
        functools.partial(_rwkv_prep_kernel, tiles_per_seq=seq // tt),
        grid=(m // tt,),
        in_specs=[pl.BlockSpec((tt, 3 * w), row),
                  pl.BlockSpec((tt, LORA_PAD), lambda i: (i, lora_blk)),
                  pl.BlockSpec((8, 3 * w), prev),
                  pl.BlockSpec((8, LORA_PAD), prev_l),
                  pl.BlockSpec((1, 3 * w), const),
                  pl.BlockSpec((1, LORA_PAD), const),
                  vec, vec, vec, vec, vec,
                  pl.BlockSpec((LORA_PAD, w), const),
                  pl.BlockSpec((LORA_PAD, w), const),
                  pl.BlockSpec((LORA_PAD, w), const)],
        out_specs=[pl.BlockSpec((tt, w), row)] * 8,
        out_shape=[out16] * 5 + [out, out16, out16],
        compiler_params=_params("parallel"),
        name="rwkv_prep",
    )(proj, proj, proj, proj, mu_rkv, mu_lora, w0, a0, k_k, k_a, r_k, w2p, a2p, g2p)


def _wkv_chunk(r, k, v, kk, b, lw, s_prev):
    c = WKV_CHUNK
    pairs = range(len(r))
    units = [(p, h) for p in pairs for h in range(2)]
    row = lax.broadcasted_iota(jnp.int32, (c, c), 0)
    col = lax.broadcasted_iota(jnp.int32, (c, c), 1)
    lower = row >= col
    strict = row > col
    eye = (row == col).astype(F32)
    low16 = lower.astype(BF16)
    head_a = lax.broadcasted_iota(jnp.int32, (c, LANES), 1) < HEAD_DIM
    nt = lambda x, y: lax.dot_general(x, y, NT_DIMS, preferred_element_type=F32)
    mm = lambda x, y: jnp.dot(x, y, preferred_element_type=F32)

    cum = [_dot_exact_lhs(low16, lw[p]) for p in pairs]
    cend = [cum[p][c - 1:c, :] for p in pairs]
    cc = [cum[p] - cum[p][c // 2 - 1:c // 2, :] for p in pairs]
    e_neg = [jnp.exp(-cc[p]) for p in pairs]
    at = [-kk[p] * jnp.exp(cc[p] - lw[p]) for p in pairs]
    rt = [r[p] * jnp.exp(cc[p]) for p in pairs]
    lhs = [jnp.concatenate([jnp.where(head_a, at[p], 0.0), jnp.where(head_a, 0.0, at[p]),
                            jnp.where(head_a, rt[p], 0.0), jnp.where(head_a, 0.0, rt[p])],
                           axis=0).astype(BF16) for p in pairs]
    rhs = [jnp.concatenate([b[p] * e_neg[p], k[p] * e_neg[p]], axis=0).astype(BF16) for p in pairs]
    gram = [nt(lhs[p], rhs[p]) for p in pairs]
    s0 = [s_prev[p].astype(BF16) for p in pairs]
    v16 = [v[p].astype(BF16) for p in pairs]
    state_u = [nt((-kk[p] * jnp.exp(cum[p] - lw[p])).astype(BF16), s0[p]) for p in pairs]
    state_y = [nt((r[p] * jnp.exp(cum[p])).astype(BF16), s0[p]) for p in pairs]

    pw = [jnp.where(strict, gram[p][h * c:(h + 1) * c, :c], 0.0) for p, h in units]
    inv = [eye + x for x in pw]
    for _ in range(6):
        pw16 = [x.astype(BF16) for x in pw]
        pw = [mm(x, x) for x in pw16]
        inv = [i + _bdot(i, x) for i, x in zip(inv, pw)]
    a_ak = [jnp.where(strict, gram[p][h * c:(h + 1) * c, c:], 0.0).astype(BF16) for p, h in units]
    rhs_u = [state_u[p] + mm(a_ak[2 * p + h], v16[p]) for p, h in units]
    us = [_bdot(i, x) for i, x in zip(inv, rhs_u)]
    u16 = [jnp.where(head_a, us[2 * p], us[2 * p + 1]).astype(BF16) for p in pairs]

    a_rb = [jnp.where(lower, gram[p][(2 + h) * c:(3 + h) * c, :c], 0.0).astype(BF16)
            for p, h in units]
    a_rk = [jnp.where(lower, gram[p][(2 + h) * c:(3 + h) * c, c:], 0.0).astype(BF16)
            for p, h in units]
    yh = [mm(a_rb[2 * p + h], u16[p]) + mm(a_rk[2 * p + h], v16[p]) for p, h in units]
    ys = [state_y[p] + jnp.where(head_a, yh[2 * p], yh[2 * p + 1]) for p in pairs]

    e_end = [jnp.exp(cend[p] - cum[p]) for p in pairs]
    upd = [lax.dot_general(jnp.concatenate([u16[p], v16[p]], axis=0),
                           jnp.concatenate([b[p] * e_end[p], k[p] * e_end[p]], axis=0).astype(BF16),
                           TN_DIMS, preferred_element_type=F32) for p in pairs]
    srow = lax.broadcasted_iota(jnp.int32, (LANES, LANES), 0) // HEAD_DIM
    scol = lax.broadcasted_iota(jnp.int32, (LANES, LANES), 1) // HEAD_DIM
    same_head = srow == scol
    s_new = [s_prev[p] * jnp.exp(cend[p]) + jnp.where(same_head, upd[p], 0.0) for p in pairs]
    return ys, s_new


def _wkv_kernel(r_ref, k_ref, v_ref, kk_ref, b_ref, lw_ref, y_ref, s_ref):
    @pl.when(pl.program_id(2) == 0)
    def _():
        s_ref[...] = jnp.zeros_like(s_ref)

    n_pairs = s_ref.shape[0]
    lanes = [slice(p * LANES, (p + 1) * LANES) for p in range(n_pairs)]
    load = lambda ref: [ref[:, sl].astype(F32) for sl in lanes]
    ys, s_new = _wkv_chunk(load(r_ref), load(k_ref), load(v_ref), load(kk_ref), load(b_ref),
                           load(lw_ref), [s_ref[p] for p in range(n_pairs)])
    for p in range(n_pairs):
        y_ref[:, lanes[p]] = ys[p]
        s_ref[p] = s_new[p]


def _wkv_scan(r, k, v, kk, b, lw, batch, seq, *, pairs=8):
    m, w = r.shape
    c = WKV_CHUNK
    nchunk = seq // c
    spec = pl.BlockSpec((c, pairs * LANES), lambda bi, p, ci: (bi * nchunk + ci, p))
    return pl.pallas_call(
        _wkv_kernel,
        grid=(batch, w // (pairs * LANES), nchunk),
        in_specs=[spec] * 6,
        out_specs=spec,
        out_shape=jax.ShapeDtypeStruct((m, w), F32),
        scratch_shapes=[pltpu.VMEM((pairs, LANES, LANES), F32)],
        compiler_params=_params("parallel", "parallel", "arbitrary"),
        name="wkv_scan",
    )(r, k, v, kk, b, lw)


def _mix_post_kernel(y_ref, g_ref, bonus_ref, lnw_ref, lnb_ref, bg_ref, cg_ref, hv_ref,
                     cgp_ref, hvp_ref, cw_ref, o_ref, *, tiles_per_seq, lane_pad):
    first = (pl.program_id(0) % tiles_per_seq) == 0
    w = y_ref.shape[1]
    y = y_ref[...]
    mean = _head_sum(y) * (1.0 / HEAD_DIM)
    d = y - mean
    var = _head_sum(d * d) * (1.0 / HEAD_DIM)
    yn = d * lax.rsqrt(var + GN_EPS) * lnw_ref[...] + lnb_ref[...]
    o_ref[:, :w] = ((yn + bonus_ref[...].astype(F32)) * g_ref[...].astype(F32)).astype(BF16)
    z = cg_ref[...] * hv_ref[...]
    zp = cgp_ref[...] * hvp_ref[...]
    cw = cw_ref[...]
    zc = (cw[0:1, :] * _shift_rows(z, zp, 2, first) + cw[1:2, :] * _shift_rows(z, zp, 1, first)
          + cw[2:3, :] * z)
    o_ref[:, w:] = (bg_ref[...] * zc)[:, lane_pad:lane_pad + w].astype(BF16)


def _mix_post(y, g, bonus, ln_w, ln_b, proj, conv_w, conv_start, seq, *, tt=128):
    m, w = y.shape
    lane_pad = conv_start % LANES
    win = w + (LANES if lane_pad else 0)
    starts = [conv_start - lane_pad + j * w for j in range(3)]
    assert starts[2] + win <= proj.shape[1], "proj must be lane-padded so every window is in bounds"
    cw = jnp.pad(conv_w.T, ((0, 8 - CONV_K), (lane_pad, win - w - lane_pad)))
    row = lambda i: (i, 0)
    const = lambda i: (0, 0)
    cur = lambda s: pl.BlockSpec((pl.Element(tt), pl.Element(win)), lambda i: (i * tt, s))
    prev = lambda s: pl.BlockSpec((pl.Element(8), pl.Element(win)),
                                  lambda i: (pl.multiple_of(jnp.maximum(i * tt - 8, 0), 8), s))
    return pl.pallas_call(
        functools.partial(_mix_post_kernel, tiles_per_seq=seq // tt, lane_pad=lane_pad),
        grid=(m // tt,),
        in_specs=[pl.BlockSpec((tt, w), row)] * 3
        + [pl.BlockSpec((1, w), const)] * 2
        + [cur(s) for s in starts]
        + [prev(starts[1]), prev(starts[2]), pl.BlockSpec((8, win), const)],
        out_specs=pl.BlockSpec((tt, 2 * w), row),
        out_shape=jax.ShapeDtypeStruct((m, 2 * w), BF16),
        compiler_params=_params("parallel"),
        name="mix_post",
    )(y, g, bonus, ln_w, ln_b, proj, proj, proj, proj, proj, cw)


def _matmul_residual_kernel(a_ref, w_ref, res_ref, o_ref):
    o_ref[...] = res_ref[...] + jnp.dot(a_ref[...], w_ref[...].astype(BF16),
                                        preferred_element_type=F32)


def _matmul_residual(a_bf16, w_stack, layer, res, *, tm=1024, tn=512):
    m, kd = a_bf16.shape
    n = w_stack.shape[2]
    return pl.pallas_call(
        _matmul_residual_kernel,
        grid=(m // tm, n // tn),
        in_specs=[pl.BlockSpec((tm, kd), lambda i, j: (i, 0)),
                  pl.BlockSpec((None, kd, tn), lambda i, j: (layer, 0, j)),
                  pl.BlockSpec((tm, tn), lambda i, j: (i, j))],
        out_specs=pl.BlockSpec((tm, tn), lambda i, j: (i, j)),
        out_shape=jax.ShapeDtypeStruct((m, n), F32),
        compiler_params=_params("parallel", "arbitrary"),
        name="matmul_residual",
    )(a_bf16, w_stack, res)


def _top_values(s, count):
    tops = []
    for _ in range(count):
        mx = jnp.max(s, axis=0, keepdims=True)
        tops.append(mx)
        s = jnp.where(s >= mx, -jnp.inf, s)
    return tops


def _peer_select_kernel(q_ref, keys_ref, th_o, e1_o, s2_o, st_o):
    q = q_ref[...]
    half = q.shape[1] // 2
    sc = []
    for c in range(2):
        qc = q[:, c * half:(c + 1) * half]
        sc.append(lax.dot_general(keys_ref[0, c].astype(BF16), qc.astype(BF16), NT_DIMS,
                                  preferred_element_type=F32))
    top1 = _top_values(sc[0], PEER_TOPK)
    top2 = _top_values(sc[1], PEER_TOPK)
    t1 = jnp.concatenate(top1, axis=0)
    t2 = jnp.concatenate(top2, axis=0)
    row8 = lax.broadcasted_iota(jnp.int32, (8, t2.shape[1]), 0)
    groups = [top1[0] + t2]
    for m_ in range(1, 8):
        n_keep = PEER_TOPK // (m_ + 1)
        g8 = top1[m_] + t2[:8]
        groups.append(g8 if n_keep >= 8 else jnp.where(row8 < n_keep, g8, -jnp.inf))
    groups.append(t1[8:] + top2[0])
    cand = jnp.concatenate(groups, axis=0)
    best = _top_values(cand, PEER_TOPK)
    cmax = best[0]
    z = jnp.exp(best[0] - cmax)
    for bv in best[1:]:
        z = z + jnp.exp(bv - cmax)
    tau = best[-1]
    th = jnp.full_like(sc[0], jnp.inf)
    for m_ in range(PEER_TOPK):
        th_m = jnp.min(jnp.where(top1[m_] + t2 >= tau, t2, jnp.inf), axis=0, keepdims=True)
        th = jnp.where(sc[0] == top1[m_], th_m, th)
    th_o[0] = th
    e1_o[0] = jnp.exp(sc[0] - top1[0]) * (1.0 / z)
    s2_o[0] = sc[1]
    st_o[0] = jnp.concatenate([top2[0], jnp.zeros((STAT_ROWS - 1, z.shape[1]), F32)], axis=0)


def _peer_select(q, sub_keys, *, tt=512):
    assert PEER_TOPK == 16, "candidate groups in _peer_select_kernel are laid out for K = 16"
    m = q.shape[0]
    heads, _, nk, half = sub_keys.shape
    big = jax.ShapeDtypeStruct((heads, nk, m), F32)
    bspec = pl.BlockSpec((1, nk, tt), lambda i, h: (h, 0, i))
    return pl.pallas_call(
        _peer_select_kernel,
        grid=(m // tt, heads),
        in_specs=[pl.BlockSpec((tt, 2 * half), lambda i, h: (i, h)),
                  pl.BlockSpec((1, 2, nk, half), lambda i, h: (h, 0, 0, 0))],
        out_specs=[bspec] * 3 + [pl.BlockSpec((1, STAT_ROWS, tt), lambda i, h: (h, 0, i))],
        out_shape=[big] * 3 + [jax.ShapeDtypeStruct((heads, STAT_ROWS, m), F32)],
        compiler_params=_params("parallel", "arbitrary"),
        name="peer_select",
    )(q, sub_keys)


def _peer_mlp_kernel(x_ref, down_ref, up_ref, th_ref, e1_ref, s2_ref, st_ref, o_ref, e2_scr,
                     gate_scr):
    heads = s2_ref.shape[0]
    sub = 2 * N_KEYS

    @pl.when(pl.program_id(1) == 0)
    def _():
        o_ref[...] = jnp.zeros_like(o_ref)
        for h in range(heads):
            e2_scr[h] = jnp.exp(s2_ref[h] - st_ref[h, 0:1, :])

    tm = x_ref.shape[0]
    te = down_ref.shape[0]
    first_row = pl.program_id(1) * (te // N_KEYS)

    def gate_unit(ii, t0):
        tk = slice(t0, t0 + LANES)
        acc = None
        for h in range(heads):
            th = th_ref[h, pl.ds(first_row + ii, 1), :][:, tk]
            e1 = e1_ref[h, pl.ds(first_row + ii, 1), :][:, tk]
            term = jnp.where(s2_ref[h, :, tk] >= th, e1 * e2_scr[h, :, tk], 0.0)
            acc = term if acc is None else acc + term
        gate_scr[tk, ii * N_KEYS:(ii + 1) * N_KEYS] = acc.T

    for ii in range(te // N_KEYS):
        for t0 in range(0, tm, LANES):
            gate_unit(ii, t0)
    x = x_ref[...]
    hid = []
    for j in range(te // sub):
        cols = slice(j * sub, (j + 1) * sub)
        act = lax.dot_general(x, down_ref[cols, :], NT_DIMS, preferred_element_type=F32)
        gelu = 0.5 * act * (1.0 + lax.erf(act * (2.0 ** -0.5)))
        hid.append((gelu * gate_scr[:, cols]).astype(BF16))
    o_ref[...] += jnp.dot(jnp.concatenate(hid, axis=1), up_ref[...], preferred_element_type=F32)


def _peer_mlp(xn_bf16, down_bf16, up_bf16, th, e1, s2, stats, *, tm=512, te=512):
    m, d = xn_bf16.shape
    ne = down_bf16.shape[0]
    heads, nk, _ = s2.shape
    big = pl.BlockSpec((heads, nk, tm), lambda i, e: (0, 0, i))
    return pl.pallas_call(
        _peer_mlp_kernel,
        grid=(m // tm, ne // te),
        in_specs=[pl.BlockSpec((tm, d), lambda i, e: (i, 0)),
                  pl.BlockSpec((te, d), lambda i, e: (e, 0)),
                  pl.BlockSpec((te, d), lambda i, e: (e, 0)),
                  big, big, big,
                  pl.BlockSpec((heads, STAT_ROWS, tm), lambda i, e: (0, 0, i))],
        out_specs=pl.BlockSpec((tm, d), lambda i, e: (i, 0)),
        out_shape=jax.ShapeDtypeStruct((m, d), F32),
        scratch_shapes=[pltpu.VMEM((heads, nk, tm), F32), pltpu.VMEM((tm, te), F32)],
        compiler_params=pltpu.CompilerParams(dimension_semantics=("parallel", "arbitrary"),
                                             vmem_limit_bytes=PEER_VMEM_LIMIT),
        name="peer_mlp",
    )(xn_bf16, down_bf16, up_bf16, th, e1, s2, stats)


def _final_norm_kernel(h_ref, p_ref, nw_ref, o_ref):
    x = h_ref[...] + p_ref[...]
    y = x * lax.rsqrt(jnp.mean(x * x, axis=-1, keepdims=True) + NORM_EPS)
    o_ref[...] = y * nw_ref[...]


def _final_norm(h, p, nw, *, tt=256):
    m, d = h.shape
    row = lambda i: (i, 0)
    return pl.pallas_call(
        _final_norm_kernel,
        grid=(m // tt,),
        in_specs=[pl.BlockSpec((tt, d), row), pl.BlockSpec((tt, d), row),
                  pl.BlockSpec((1, d), lambda i: (0, 0))],
        out_specs=pl.BlockSpec((tt, d), row),
        out_shape=jax.ShapeDtypeStruct((m, d), F32),
        compiler_params=_params("parallel"),
        name="final_norm",
    )(h, p, nw.reshape(1, d))


def _pad_rows(mat, start, total):
    return jnp.pad(mat, ((start, total - start - mat.shape[0]), (0, 0)))


def kernel(x, norm1_w, w_in, mu_shift, w0, w2, a0, a2, g2, k_k, k_a, r_k, ln_x_w, ln_x_b,
           conv_w, w_out, norm2_w, w_q, sub_keys, expert_down, expert_up, norm_f_w):
    batch, seq, d = x.shape
    depth = w_in.shape[0]
    w = w0.shape[1]
    rwkv_in = 3 * w + DECAY_LORA + ICLR_LORA + GATE_LORA
    lora_w = rwkv_in - 3 * w
    m = batch * seq
    h = x.reshape(m, d)
    row = lambda t: t.reshape(1, -1)
    for l in range(depth):
        proj = _norm_matmul(h, norm1_w[l], jnp.swapaxes(w_in, 1, 2), l, emit_xn=False,
                            w_rows_are_outputs=True, tm=1024)
        mu = mu_shift[l]
        mu_lora = jnp.pad(mu[3 * w:], (0, LORA_PAD - lora_w))
        w2p = _pad_rows(w2[l], 0, LORA_PAD).astype(BF16)
        a2p = _pad_rows(a2[l], DECAY_LORA, LORA_PAD).astype(BF16)
        g2p = _pad_rows(g2[l], DECAY_LORA + ICLR_LORA, LORA_PAD).astype(BF16)
        r, k2, v, kk, bvec, lw, g, bonus = _rwkv_prep(
            proj, seq, w, row(mu[:3 * w]), row(mu_lora), row(w0[l]), row(a0[l]), row(k_k[l]),
            row(k_a[l]), row(r_k[l]), w2p, a2p, g2p)
        y = _wkv_scan(r, k2, v, kk, bvec, lw, batch, seq)
        mixed = _mix_post(y, g, bonus, row(ln_x_w[l]), row(ln_x_b[l]), proj, conv_w[l], rwkv_in,
                          seq)
        h = _matmul_residual(mixed, w_out, l, h)
        q, xn = _norm_matmul(h, norm2_w[l], w_q, l, emit_xn=True, w_rows_are_outputs=False,
                             tm=1024, tn=256)
        th, e1, s2, stats = _peer_select(q, sub_keys[l])
        peer = _peer_mlp(xn, expert_down[l].astype(BF16), expert_up[l].astype(BF16),
                         th, e1, s2, stats)
        if l + 1 < depth:
            h = h + peer
    return _final_norm(h, peer, norm_f_w).reshape(batch, seq, d)
```

```python
import functools

import jax
import jax.numpy as jnp
from jax import lax
from jax.experimental import pallas as pl
from jax.experimental.pallas import tpu as pltpu

F32 = jnp.float32
BF16 = jnp.bfloat16

HEAD_DIM = 64
LANES = 128
NORM_EPS = 1e-6
GN_EPS = 64e-5
DECAY_LORA = 96
ICLR_LORA = 96
GATE_LORA = 256
LORA_PAD = 512
CONV_K = 3
PEER_HEADS = 8
N_KEYS = 128
PEER_TOPK = 16
WKV_CHUNK = 128
STAT_ROWS = 8
NORM_ROWS = 128
VMEM_LIMIT = 56 * 1024 * 1024
PEER_VMEM_LIMIT = 60 * 1024 * 1024

NT_DIMS = (((1,), (1,)), ((), ()))
TN_DIMS = (((0,), (0,)), ((), ()))


def _params(*sem):
    return pltpu.CompilerParams(dimension_semantics=sem, vmem_limit_bytes=VMEM_LIMIT)


def _bdot(a, b):
    return jnp.dot(a.astype(BF16), b.astype(BF16), preferred_element_type=F32)


def _split3(x):
    hi = x.astype(BF16)
    r1 = x - hi.astype(F32)
    mid = r1.astype(BF16)
    lo = (r1 - mid.astype(F32)).astype(BF16)
    return hi, mid, lo


def _dot_exact_lhs(a_bf16, x):
    hi, mid, lo = _split3(x)
    d = lambda t: jnp.dot(a_bf16, t, preferred_element_type=F32)
    return d(hi) + d(mid) + d(lo)


def _dot_exact_rhs(x, b_bf16):
    hi, mid, lo = _split3(x)
    d = lambda t: jnp.dot(t, b_bf16, preferred_element_type=F32)
    return d(hi) + d(mid) + d(lo)


def _head_ones():
    r = lax.broadcasted_iota(jnp.int32, (LANES, LANES), 0) // HEAD_DIM
    c = lax.broadcasted_iota(jnp.int32, (LANES, LANES), 1) // HEAD_DIM
    return (r == c).astype(BF16)


def _head_sum(x):
    ones = _head_ones()
    parts = [_dot_exact_rhs(x[:, j:j + LANES], ones) for j in range(0, x.shape[1], LANES)]
    return jnp.concatenate(parts, axis=1)


def _norm_matmul_kernel(x_ref, nw_ref, w_ref, o_ref, *rest, emit_xn, w_rows_are_outputs, n_valid):
    xn_scr = rest[-1]

    @pl.when(pl.program_id(1) == 0)
    def _():
        for r0 in range(0, x_ref.shape[0], NORM_ROWS):
            x = x_ref[r0:r0 + NORM_ROWS, :]
            y = x * lax.rsqrt(jnp.mean(x * x, axis=-1, keepdims=True) + NORM_EPS)
            xn_scr[r0:r0 + NORM_ROWS, :] = (y * nw_ref[...]).astype(BF16)

    w = w_ref[...].astype(BF16)
    if w_rows_are_outputs:
        out = lax.dot_general(xn_scr[...], w, NT_DIMS, preferred_element_type=F32)
    else:
        out = jnp.dot(xn_scr[...], w, preferred_element_type=F32)
    tn = o_ref.shape[1]
    if n_valid % tn:
        col = pl.program_id(1) * tn + lax.broadcasted_iota(jnp.int32, out.shape, 1)
        out = jnp.where(col < n_valid, out, 0.0)
    o_ref[...] = out
    if emit_xn:
        @pl.when(pl.program_id(1) == 0)
        def _():
            rest[0][...] = xn_scr[...]


def _norm_matmul(x, nw, w_stack, layer, *, emit_xn, w_rows_are_outputs, tm, tn=512):
    m, d = x.shape
    n = w_stack.shape[1] if w_rows_are_outputs else w_stack.shape[2]
    n_blocks = pl.cdiv(n, tn)
    if w_rows_are_outputs:
        w_spec = pl.BlockSpec((None, tn, d), lambda i, j: (layer, j, 0))
    else:
        w_spec = pl.BlockSpec((None, d, tn), lambda i, j: (layer, 0, j))
    out_shape = [jax.ShapeDtypeStruct((m, n_blocks * tn), F32)]
    out_specs = [pl.BlockSpec((tm, tn), lambda i, j: (i, j))]
    if emit_xn:
        out_shape.append(jax.ShapeDtypeStruct((m, d), BF16))
        out_specs.append(pl.BlockSpec((tm, d), lambda i, j: (i, 0)))
    res = pl.pallas_call(
        functools.partial(_norm_matmul_kernel, emit_xn=emit_xn,
                          w_rows_are_outputs=w_rows_are_outputs, n_valid=n),
        grid=(m // tm, n_blocks),
        in_specs=[pl.BlockSpec((tm, d), lambda i, j: (i, 0), pipeline_mode=pl.Buffered(1)),
                  pl.BlockSpec((1, d), lambda i, j: (0, 0)),
                  w_spec],
        out_specs=out_specs,
        out_shape=out_shape,
        scratch_shapes=[pltpu.VMEM((tm, d), BF16)],
        compiler_params=_params("parallel", "arbitrary"),
        name="norm_matmul",
    )(x, nw.reshape(1, d), w_stack)
    return res if emit_xn else res[0]


def _shift_rows(x, prev8, n, first):
    rows = lax.broadcasted_iota(jnp.int32, x.shape, 0)
    out = pltpu.roll(x, n, 0)
    for j in range(n):
        fill = jnp.where(first, 0.0, prev8[8 - n + j:8 - n + j + 1, :])
        out = jnp.where(rows == j, fill, out)
    return out


def _rwkv_prep_kernel(p_ref, pl_ref, pp_ref, ppl_ref, mu_ref, mul_ref, w0_ref, a0_ref,
                      kk_ref, ka_ref, rk_ref, w2_ref, a2_ref, g2_ref,
                      r_o, k_o, v_o, kk_o, b_o, lw_o, g_o, bonus_o, *, tiles_per_seq):
    first = (pl.program_id(0) % tiles_per_seq) == 0
    w = r_o.shape[1]
    p = p_ref[...]
    ps = p + (_shift_rows(p, pp_ref[...], 1, first) - p) * mu_ref[...]
    q = pl_ref[...]
    qs = q + (_shift_rows(q, ppl_ref[...], 1, first) - q) * mul_ref[...]
    r = ps[:, :w]
    k = ps[:, w:2 * w]
    v = ps[:, 2 * w:]
    z = w0_ref[...] + _bdot(jnp.tanh(qs), w2_ref[...])
    w_log = jnp.minimum(z, 0.0) - jnp.log(1.0 + jnp.exp(-jnp.abs(z))) - 0.5
    a = jax.nn.sigmoid(a0_ref[...] + _bdot(qs, a2_ref[...]))
    g = _bdot(jax.nn.sigmoid(qs), g2_ref[...])
    kk = k * kk_ref[...]
    kk = kk * lax.rsqrt(jnp.maximum(_head_sum(kk * kk), 1e-24))
    k2 = k * (1.0 + (a - 1.0) * ka_ref[...])
    r_o[...] = r.astype(BF16)
    k_o[...] = k2.astype(BF16)
    v_o[...] = v.astype(BF16)
    kk_o[...] = kk.astype(BF16)
    b_o[...] = (kk * a).astype(BF16)
    lw_o[...] = -jnp.exp(w_log)
    g_o[...] = g.astype(BF16)
    bonus_o[...] = (_head_sum(r * k2 * rk_ref[...]) * v).astype(BF16)


def _rwkv_prep(proj, seq, w, mu_rkv, mu_lora, w0, a0, k_k, k_a, r_k, w2p, a2p, g2p, *, tt=128):
    m = proj.shape[0]
    lora_blk = (3 * w) // LORA_PAD
    row = lambda i: (i, 0)
    prev = lambda i: (jnp.maximum(i * (tt // 8) - 1, 0), 0)
    prev_l = lambda i: (jnp.maximum(i * (tt // 8) - 1, 0), lora_blk)
    const = lambda i: (0, 0)
    vec = pl.BlockSpec((1, w), const)
    out = jax.ShapeDtypeStruct((m, w), F32)
    out16 = jax.ShapeDtypeStruct((m, w), BF16)
    return pl.pallas_call(
        functools.partial(_rwkv_prep_kernel, tiles_per_seq=seq // tt),
        grid=(m // tt,),
        in_specs=[pl.BlockSpec((tt, 3 * w), row),
                  pl.BlockSpec((tt, LORA_PAD), lambda i: (i, lora_blk)),
                  pl.BlockSpec((8, 3 * w), prev),
                  pl.BlockSpec((8, LORA_PAD), prev_l),
                  pl.BlockSpec((1, 3 * w), const),
                  pl.BlockSpec((1, LORA_PAD), const),
                  vec, vec, vec, vec, vec,
                  pl.BlockSpec((LORA_PAD, w), const),
                  pl.BlockSpec((LORA_PAD, w), const),
                  pl.BlockSpec((LORA_PAD, w), const)],
        out_specs=[pl.BlockSpec((tt, w), row)] * 8,
        out_shape=[out16] * 5 + [out, out16, out16],
        compiler_params=_params("parallel"),
        name="rwkv_prep",
    )(proj, proj, proj, proj, mu_rkv, mu_lora, w0, a0, k_k, k_a, r_k, w2p, a2p, g2p)


def _wkv_chunk(r, k, v, kk, b, lw, s_prev):
    c = WKV_CHUNK
    pairs = range(len(r))
    units = [(p, h) for p in pairs for h in range(2)]
    row = lax.broadcasted_iota(jnp.int32, (c, c), 0)
    col = lax.broadcasted_iota(jnp.int32, (c, c), 1)
    lower = row >= col
    strict = row > col
    eye = (row == col).astype(F32)
    low16 = lower.astype(BF16)
    head_a = lax.broadcasted_iota(jnp.int32, (c, LANES), 1) < HEAD_DIM
    nt = lambda x, y: lax.dot_general(x, y, NT_DIMS, preferred_element_type=F32)
    mm = lambda x, y: jnp.dot(x, y, preferred_element_type=F32)

    cum = [_dot_exact_lhs(low16, lw[p]) for p in pairs]
    cend = [cum[p][c - 1:c, :] for p in pairs]
    cc = [cum[p] - cum[p][c // 2 - 1:c // 2, :] for p in pairs]
    e_neg = [jnp.exp(-cc[p]) for p in pairs]
    at = [-kk[p] * jnp.exp(cc[p] - lw[p]) for p in pairs]
    rt = [r[p] * jnp.exp(cc[p]) for p in pairs]
    lhs = [jnp.concatenate([jnp.where(head_a, at[p], 0.0), jnp.where(head_a, 0.0, at[p]),
                            jnp.where(head_a, rt[p], 0.0), jnp.where(head_a, 0.0, rt[p])],
                           axis=0).astype(BF16) for p in pairs]
    rhs = [jnp.concatenate([b[p] * e_neg[p], k[p] * e_neg[p]], axis=0).astype(BF16) for p in pairs]
    gram = [nt(lhs[p], rhs[p]) for p in pairs]
    s0 = [s_prev[p].astype(BF16) for p in pairs]
    v16 = [v[p].astype(BF16) for p in pairs]
    state_u = [nt((-kk[p] * jnp.exp(cum[p] - lw[p])).astype(BF16), s0[p]) for p in pairs]
    state_y = [nt((r[p] * jnp.exp(cum[p])).astype(BF16), s0[p]) for p in pairs]

    pw = [jnp.where(strict, gram[p][h * c:(h + 1) * c, :c], 0.0) for p, h in units]
    inv = [eye + x for x in pw]
    for _ in range(6):
        pw16 = [x.astype(BF16) for x in pw]
        pw = [mm(x, x) for x in pw16]
        inv = [i + _bdot(i, x) for i, x in zip(inv, pw)]
    a_ak = [jnp.where(strict, gram[p][h * c:(h + 1) * c, c:], 0.0).astype(BF16) for p, h in units]
    rhs_u = [state_u[p] + mm(a_ak[2 * p + h], v16[p]) for p, h in units]
    us = [_bdot(i, x) for i, x in zip(inv, rhs_u)]
    u16 = [jnp.where(head_a, us[2 * p], us[2 * p + 1]).astype(BF16) for p in pairs]

    a_rb = [jnp.where(lower, gram[p][(2 + h) * c:(3 + h) * c, :c], 0.0).astype(BF16)
            for p, h in units]
    a_rk = [jnp.where(lower, gram[p][(2 + h) * c:(3 + h) * c, c:], 0.0).astype(BF16)
            for p, h in units]
    yh = [mm(a_rb[2 * p + h], u16[p]) + mm(a_rk[2 * p + h], v16[p]) for p, h in units]
    ys = [state_y[p] + jnp.where(head_a, yh[2 * p], yh[2 * p + 1]) for p in pairs]

    e_end = [jnp.exp(cend[p] - cum[p]) for p in pairs]
    upd = [lax.dot_general(jnp.concatenate([u16[p], v16[p]], axis=0),
                           jnp.concatenate([b[p] * e_end[p], k[p] * e_end[p]], axis=0).astype(BF16),
                           TN_DIMS, preferred_element_type=F32) for p in pairs]
    srow = lax.broadcasted_iota(jnp.int32, (LANES, LANES), 0) // HEAD_DIM
    scol = lax.broadcasted_iota(jnp.int32, (LANES, LANES), 1) // HEAD_DIM
    same_head = srow == scol
    s_new = [s_prev[p] * jnp.exp(cend[p]) + jnp.where(same_head, upd[p], 0.0) for p in pairs]
    return ys, s_new


def _wkv_kernel(r_ref, k_ref, v_ref, kk_ref, b_ref, lw_ref, down_ref, up_ref,
                y_ref, down16_ref, up16_ref, s_ref):
    @pl.when(pl.program_id(2) == 0)
    def _():
        s_ref[...] = jnp.zeros_like(s_ref)

    down16_ref[...] = down_ref[...].astype(BF16)
    up16_ref[...] = up_ref[...].astype(BF16)

    n_pairs = s_ref.shape[0]
    lanes = [slice(p * LANES, (p + 1) * LANES) for p in range(n_pairs)]
    load = lambda ref: [ref[:, sl].astype(F32) for sl in lanes]
    ys, s_new = _wkv_chunk(load(r_ref), load(k_ref), load(v_ref), load(kk_ref), load(b_ref),
                           load(lw_ref), [s_ref[p] for p in range(n_pairs)])
    for p in range(n_pairs):
        y_ref[:, lanes[p]] = ys[p]
        s_ref[p] = s_new[p]


def _wkv_scan(r, k, v, kk, b, lw, batch, seq, down_stack, up_stack, layer, *, pairs=8):
    m, w = r.shape
    c = WKV_CHUNK
    nchunk = seq // c
    groups = w // (pairs * LANES)
    n_exp, d = down_stack.shape[1:]
    slab = n_exp // (batch * groups * nchunk)
    assert slab * batch * groups * nchunk == n_exp and slab % 16 == 0
    step = lambda bi, p, ci: (bi * groups + p) * nchunk + ci
    spec = pl.BlockSpec((c, pairs * LANES), lambda bi, p, ci: (bi * nchunk + ci, p))
    tab_in = pl.BlockSpec((None, slab, d), lambda bi, p, ci: (layer, step(bi, p, ci), 0))
    tab_out = pl.BlockSpec((slab, d), lambda bi, p, ci: (step(bi, p, ci), 0))
    tab16 = jax.ShapeDtypeStruct((n_exp, d), BF16)
    return pl.pallas_call(
        _wkv_kernel,
        grid=(batch, groups, nchunk),
        in_specs=[spec] * 6 + [tab_in] * 2,
        out_specs=[spec, tab_out, tab_out],
        out_shape=[jax.ShapeDtypeStruct((m, w), F32), tab16, tab16],
        scratch_shapes=[pltpu.VMEM((pairs, LANES, LANES), F32)],
        compiler_params=_params("parallel", "parallel", "arbitrary"),
        name="wkv_scan",
    )(r, k, v, kk, b, lw, down_stack, up_stack)


def _mix_post_kernel(y_ref, g_ref, bonus_ref, lnw_ref, lnb_ref, bg_ref, cg_ref, hv_ref,
                     cgp_ref, hvp_ref, cw_ref, o_ref, *, tiles_per_seq, lane_pad):
    first = (pl.program_id(0) % tiles_per_seq) == 0
    w = y_ref.shape[1]
    y = y_ref[...]
    mean = _head_sum(y) * (1.0 / HEAD_DIM)
    d = y - mean
    var = _head_sum(d * d) * (1.0 / HEAD_DIM)
    yn = d * lax.rsqrt(var + GN_EPS) * lnw_ref[...] + lnb_ref[...]
    o_ref[:, :w] = ((yn + bonus_ref[...].astype(F32)) * g_ref[...].astype(F32)).astype(BF16)
    z = cg_ref[...] * hv_ref[...]
    zp = cgp_ref[...] * hvp_ref[...]
    cw = cw_ref[...]
    zc = (cw[0:1, :] * _shift_rows(z, zp, 2, first) + cw[1:2, :] * _shift_rows(z, zp, 1, first)
          + cw[2:3, :] * z)
    o_ref[:, w:] = (bg_ref[...] * zc)[:, lane_pad:lane_pad + w].astype(BF16)


def _mix_post(y, g, bonus, ln_w, ln_b, proj, conv_w, conv_start, seq, *, tt=128):
    m, w = y.shape
    lane_pad = conv_start % LANES
    win = w + (LANES if lane_pad else 0)
    starts = [conv_start - lane_pad + j * w for j in range(3)]
    assert starts[2] + win <= proj.shape[1], "proj must be lane-padded so every window is in bounds"
    cw = jnp.pad(conv_w.T, ((0, 8 - CONV_K), (lane_pad, win - w - lane_pad)))
    row = lambda i: (i, 0)
    const = lambda i: (0, 0)
    cur = lambda s: pl.BlockSpec((pl.Element(tt), pl.Element(win)), lambda i: (i * tt, s))
    prev = lambda s: pl.BlockSpec((pl.Element(8), pl.Element(win)),
                                  lambda i: (pl.multiple_of(jnp.maximum(i * tt - 8, 0), 8), s))
    return pl.pallas_call(
        functools.partial(_mix_post_kernel, tiles_per_seq=seq // tt, lane_pad=lane_pad),
        grid=(m // tt,),
        in_specs=[pl.BlockSpec((tt, w), row)] * 3
        + [pl.BlockSpec((1, w), const)] * 2
        + [cur(s) for s in starts]
        + [prev(starts[1]), prev(starts[2]), pl.BlockSpec((8, win), const)],
        out_specs=pl.BlockSpec((tt, 2 * w), row),
        out_shape=jax.ShapeDtypeStruct((m, 2 * w), BF16),
        compiler_params=_params("parallel"),
        name="mix_post",
    )(y, g, bonus, ln_w, ln_b, proj, proj, proj, proj, proj, cw)


def _matmul_residual_kernel(a_ref, w_ref, res_ref, o_ref):
    o_ref[...] = res_ref[...] + jnp.dot(a_ref[...], w_ref[...].astype(BF16),
                                        preferred_element_type=F32)


def _matmul_residual(a_bf16, w_stack, layer, res, *, tm=1024, tn=512):
    m, kd = a_bf16.shape
    n = w_stack.shape[2]
    return pl.pallas_call(
        _matmul_residual_kernel,
        grid=(m // tm, n // tn),
        in_specs=[pl.BlockSpec((tm, kd), lambda i, j: (i, 0)),
                  pl.BlockSpec((None, kd, tn), lambda i, j: (layer, 0, j)),
                  pl.BlockSpec((tm, tn), lambda i, j: (i, j))],
        out_specs=pl.BlockSpec((tm, tn), lambda i, j: (i, j)),
        out_shape=jax.ShapeDtypeStruct((m, n), F32),
        compiler_params=_params("parallel", "arbitrary"),
        name="matmul_residual",
    )(a_bf16, w_stack, res)


def _top_values(s, count):
    tops = []
    for _ in range(count):
        mx = jnp.max(s, axis=0, keepdims=True)
        tops.append(mx)
        s = jnp.where(s >= mx, -jnp.inf, s)
    return tops


def _peer_select_kernel(q_ref, keys_ref, th_o, e1_o, s2_o, st_o):
    q = q_ref[...]
    half = q.shape[1] // 2
    sc = []
    for c in range(2):
        qc = q[:, c * half:(c + 1) * half]
        sc.append(lax.dot_general(keys_ref[0, c].astype(BF16), qc.astype(BF16), NT_DIMS,
                                  preferred_element_type=F32))
    top1 = _top_values(sc[0], PEER_TOPK)
    top2 = _top_values(sc[1], PEER_TOPK)
    t1 = jnp.concatenate(top1, axis=0)
    t2 = jnp.concatenate(top2, axis=0)
    row8 = lax.broadcasted_iota(jnp.int32, (8, t2.shape[1]), 0)
    groups = [top1[0] + t2]
    for m_ in range(1, 8):
        n_keep = PEER_TOPK // (m_ + 1)
        g8 = top1[m_] + t2[:8]
        groups.append(g8 if n_keep >= 8 else jnp.where(row8 < n_keep, g8, -jnp.inf))
    groups.append(t1[8:] + top2[0])
    cand = jnp.concatenate(groups, axis=0)
    best = _top_values(cand, PEER_TOPK)
    cmax = best[0]
    z = jnp.exp(best[0] - cmax)
    for bv in best[1:]:
        z = z + jnp.exp(bv - cmax)
    tau = best[-1]
    th = jnp.full_like(sc[0], jnp.inf)
    for m_ in range(PEER_TOPK):
        th_m = jnp.min(jnp.where(top1[m_] + t2 >= tau, t2, jnp.inf), axis=0, keepdims=True)
        th = jnp.where(sc[0] == top1[m_], th_m, th)
    th_o[0] = th
    e1_o[0] = jnp.exp(sc[0] - top1[0]) * (1.0 / z)
    s2_o[0] = sc[1]
    st_o[0] = jnp.concatenate([top2[0], jnp.zeros((STAT_ROWS - 1, z.shape[1]), F32)], axis=0)


def _peer_select(q, sub_keys, *, tt=512):
    assert PEER_TOPK == 16, "candidate groups in _peer_select_kernel are laid out for K = 16"
    m = q.shape[0]
    heads, _, nk, half = sub_keys.shape
    big = jax.ShapeDtypeStruct((heads, nk, m), F32)
    bspec = pl.BlockSpec((1, nk, tt), lambda i, h: (h, 0, i))
    return pl.pallas_call(
        _peer_select_kernel,
        grid=(m // tt, heads),
        in_specs=[pl.BlockSpec((tt, 2 * half), lambda i, h: (i, h)),
                  pl.BlockSpec((1, 2, nk, half), lambda i, h: (h, 0, 0, 0))],
        out_specs=[bspec] * 3 + [pl.BlockSpec((1, STAT_ROWS, tt), lambda i, h: (h, 0, i))],
        out_shape=[big] * 3 + [jax.ShapeDtypeStruct((heads, STAT_ROWS, m), F32)],
        compiler_params=_params("parallel", "arbitrary"),
        name="peer_select",
    )(q, sub_keys)


def _peer_mlp_kernel(x_ref, down_ref, up_ref, th_ref, e1_ref, s2_ref, st_ref, o_ref, e2_scr,
                     gate_scr):
    heads = s2_ref.shape[0]
    sub = 2 * N_KEYS

    @pl.when(pl.program_id(1) == 0)
    def _():
        o_ref[...] = jnp.zeros_like(o_ref)
        for h in range(heads):
            e2_scr[h] = jnp.exp(s2_ref[h] - st_ref[h, 0:1, :])

    tm = x_ref.shape[0]
    te = down_ref.shape[0]
    first_row = pl.program_id(1) * (te // N_KEYS)

    def gate_unit(ii, t0):
        tk = slice(t0, t0 + LANES)
        acc = None
        for h in range(heads):
            th = th_ref[h, pl.ds(first_row + ii, 1), :][:, tk]
            e1 = e1_ref[h, pl.ds(first_row + ii, 1), :][:, tk]
            term = jnp.where(s2_ref[h, :, tk] >= th, e1 * e2_scr[h, :, tk], 0.0)
            acc = term if acc is None else acc + term
        gate_scr[tk, ii * N_KEYS:(ii + 1) * N_KEYS] = acc.T

    for ii in range(te // N_KEYS):
        for t0 in range(0, tm, LANES):
            gate_unit(ii, t0)
    x = x_ref[...]
    hid = []
    for j in range(te // sub):
        cols = slice(j * sub, (j + 1) * sub)
        act = lax.dot_general(x, down_ref[cols, :], NT_DIMS, preferred_element_type=F32)
        gelu = 0.5 * act * (1.0 + lax.erf(act * (2.0 ** -0.5)))
        hid.append((gelu * gate_scr[:, cols]).astype(BF16))
    o_ref[...] += jnp.dot(jnp.concatenate(hid, axis=1), up_ref[...], preferred_element_type=F32)


def _peer_mlp(xn_bf16, down_bf16, up_bf16, th, e1, s2, stats, *, tm=512, te=512):
    m, d = xn_bf16.shape
    ne = down_bf16.shape[0]
    heads, nk, _ = s2.shape
    big = pl.BlockSpec((heads, nk, tm), lambda i, e: (0, 0, i))
    return pl.pallas_call(
        _peer_mlp_kernel,
        grid=(m // tm, ne // te),
        in_specs=[pl.BlockSpec((tm, d), lambda i, e: (i, 0)),
                  pl.BlockSpec((te, d), lambda i, e: (e, 0)),
                  pl.BlockSpec((te, d), lambda i, e: (e, 0)),
                  big, big, big,
                  pl.BlockSpec((heads, STAT_ROWS, tm), lambda i, e: (0, 0, i))],
        out_specs=pl.BlockSpec((tm, d), lambda i, e: (i, 0)),
        out_shape=jax.ShapeDtypeStruct((m, d), F32),
        scratch_shapes=[pltpu.VMEM((heads, nk, tm), F32), pltpu.VMEM((tm, te), F32)],
        compiler_params=pltpu.CompilerParams(dimension_semantics=("parallel", "arbitrary"),
                                             vmem_limit_bytes=PEER_VMEM_LIMIT),
        name="peer_mlp",
    )(xn_bf16, down_bf16, up_bf16, th, e1, s2, stats)


def _final_norm_kernel(h_ref, p_ref, nw_ref, o_ref):
    x = h_ref[...] + p_ref[...]
    y = x * lax.rsqrt(jnp.mean(x * x, axis=-1, keepdims=True) + NORM_EPS)
    o_ref[...] = y * nw_ref[...]


def _final_norm(h, p, nw, *, tt=256):
    m, d = h.shape
    row = lambda i: (i, 0)
    return pl.pallas_call(
        _final_norm_kernel,
        grid=(m // tt,),
        in_specs=[pl.BlockSpec((tt, d), row), pl.BlockSpec((tt, d), row),
                  pl.BlockSpec((1, d), lambda i: (0, 0))],
        out_specs=pl.BlockSpec((tt, d), row),
        out_shape=jax.ShapeDtypeStruct((m, d), F32),
        compiler_params=_params("parallel"),
        name="final_norm",
    )(h, p, nw.reshape(1, d))


def _pad_rows(mat, start, total):
    return jnp.pad(mat, ((start, total - start - mat.shape[0]), (0, 0)))


def kernel(x, norm1_w, w_in, mu_shift, w0, w2, a0, a2, g2, k_k, k_a, r_k, ln_x_w, ln_x_b,
           conv_w, w_out, norm2_w, w_q, sub_keys, expert_down, expert_up, norm_f_w):
    batch, seq, d = x.shape
    depth = w_in.shape[0]
    w = w0.shape[1]
    rwkv_in = 3 * w + DECAY_LORA + ICLR_LORA + GATE_LORA
    lora_w = rwkv_in - 3 * w
    m = batch * seq
    h = x.reshape(m, d)
    row = lambda t: t.reshape(1, -1)
    for l in range(depth):
        proj = _norm_matmul(h, norm1_w[l], jnp.swapaxes(w_in, 1, 2), l, emit_xn=False,
                            w_rows_are_outputs=True, tm=1024)
        mu = mu_shift[l]
        mu_lora = jnp.pad(mu[3 * w:], (0, LORA_PAD - lora_w))
        w2p = _pad_rows(w2[l], 0, LORA_PAD).astype(BF16)
        a2p = _pad_rows(a2[l], DECAY_LORA, LORA_PAD).astype(BF16)
        g2p = _pad_rows(g2[l], DECAY_LORA + ICLR_LORA, LORA_PAD).astype(BF16)
        r, k2, v, kk, bvec, lw, g, bonus = _rwkv_prep(
            proj, seq, w, row(mu[:3 * w]), row(mu_lora), row(w0[l]), row(a0[l]), row(k_k[l]),
            row(k_a[l]), row(r_k[l]), w2p, a2p, g2p)
        y, down16, up16 = _wkv_scan(r, k2, v, kk, bvec, lw, batch, seq, expert_down, expert_up, l)
        mixed = _mix_post(y, g, bonus, row(ln_x_w[l]), row(ln_x_b[l]), proj, conv_w[l], rwkv_in,
                          seq)
        h = _matmul_residual(mixed, w_out, l, h)
        q, xn = _norm_matmul(h, norm2_w[l], w_q, l, emit_xn=True, w_rows_are_outputs=False,
                             tm=1024, tn=256)
        th, e1, s2, stats = _peer_select(q, sub_keys[l])
        peer = _peer_mlp(xn, down16, up16, th, e1, s2, stats)
        if l + 1 < depth:
            h = h + peer
    return _final_norm(h, peer, norm_f_w).reshape(batch, seq, d)
```

```python
import functools

import jax
import jax.numpy as jnp
from jax import lax
from jax.experimental import pallas as pl
from jax.experimental.pallas import tpu as pltpu

F32 = jnp.float32
BF16 = jnp.bfloat16

HEAD_DIM = 64
LANES = 128
NORM_EPS = 1e-6
GN_EPS = 64e-5
DECAY_LORA = 96
ICLR_LORA = 96
GATE_LORA = 256
LORA_PAD = 512
CONV_K = 3
PEER_HEADS = 8
N_KEYS = 128
PEER_TOPK = 16
WKV_CHUNK = 128
STAT_ROWS = 8
NORM_ROWS = 128
VMEM_LIMIT = 56 * 1024 * 1024
PEER_VMEM_LIMIT = 60 * 1024 * 1024

NT_DIMS = (((1,), (1,)), ((), ()))
TN_DIMS = (((0,), (0,)), ((), ()))


def _params(*sem):
    return pltpu.CompilerParams(dimension_semantics=sem, vmem_limit_bytes=VMEM_LIMIT)


def _bdot(a, b):
    return jnp.dot(a.astype(BF16), b.astype(BF16), preferred_element_type=F32)


def _split3(x):
    hi = x.astype(BF16)
    r1 = x - hi.astype(F32)
    mid = r1.astype(BF16)
    lo = (r1 - mid.astype(F32)).astype(BF16)
    return hi, mid, lo


def _dot_exact_lhs(a_bf16, x):
    hi, mid, lo = _split3(x)
    d = lambda t: jnp.dot(a_bf16, t, preferred_element_type=F32)
    return d(hi) + d(mid) + d(lo)


def _dot_exact_rhs(x, b_bf16):
    hi, mid, lo = _split3(x)
    d = lambda t: jnp.dot(t, b_bf16, preferred_element_type=F32)
    return d(hi) + d(mid) + d(lo)


def _head_ones():
    r = lax.broadcasted_iota(jnp.int32, (LANES, LANES), 0) // HEAD_DIM
    c = lax.broadcasted_iota(jnp.int32, (LANES, LANES), 1) // HEAD_DIM
    return (r == c).astype(BF16)


def _head_sum(x):
    ones = _head_ones()
    parts = [_dot_exact_rhs(x[:, j:j + LANES], ones) for j in range(0, x.shape[1], LANES)]
    return jnp.concatenate(parts, axis=1)


def _norm_matmul_kernel(x_ref, nw_ref, w_ref, o_ref, *rest, emit_xn, w_rows_are_outputs, n_valid):
    xn_scr = rest[-1]

    @pl.when(pl.program_id(1) == 0)
    def _():
        for r0 in range(0, x_ref.shape[0], NORM_ROWS):
            x = x_ref[r0:r0 + NORM_ROWS, :]
            y = x * lax.rsqrt(jnp.mean(x * x, axis=-1, keepdims=True) + NORM_EPS)
            xn_scr[r0:r0 + NORM_ROWS, :] = (y * nw_ref[...]).astype(BF16)

    w = w_ref[...].astype(BF16)
    if w_rows_are_outputs:
        out = lax.dot_general(xn_scr[...], w, NT_DIMS, preferred_element_type=F32)
    else:
        out = jnp.dot(xn_scr[...], w, preferred_element_type=F32)
    tn = o_ref.shape[1]
    if n_valid % tn:
        col = pl.program_id(1) * tn + lax.broadcasted_iota(jnp.int32, out.shape, 1)
        out = jnp.where(col < n_valid, out, 0.0)
    o_ref[...] = out
    if emit_xn:
        @pl.when(pl.program_id(1) == 0)
        def _():
            rest[0][...] = xn_scr[...]


def _norm_matmul(x, nw, w_stack, layer, *, emit_xn, w_rows_are_outputs, tm, tn=512):
    m, d = x.shape
    n = w_stack.shape[1] if w_rows_are_outputs else w_stack.shape[2]
    n_blocks = pl.cdiv(n, tn)
    if w_rows_are_outputs:
        w_spec = pl.BlockSpec((None, tn, d), lambda i, j: (layer, j, 0))
    else:
        w_spec = pl.BlockSpec((None, d, tn), lambda i, j: (layer, 0, j))
    out_shape = [jax.ShapeDtypeStruct((m, n_blocks * tn), F32)]
    out_specs = [pl.BlockSpec((tm, tn), lambda i, j: (i, j))]
    if emit_xn:
        out_shape.append(jax.ShapeDtypeStruct((m, d), BF16))
        out_specs.append(pl.BlockSpec((tm, d), lambda i, j: (i, 0)))
    res = pl.pallas_call(
        functools.partial(_norm_matmul_kernel, emit_xn=emit_xn,
                          w_rows_are_outputs=w_rows_are_outputs, n_valid=n),
        grid=(m // tm, n_blocks),
        in_specs=[pl.BlockSpec((tm, d), lambda i, j: (i, 0), pipeline_mode=pl.Buffered(1)),
                  pl.BlockSpec((1, d), lambda i, j: (0, 0)),
                  w_spec],
        out_specs=out_specs,
        out_shape=out_shape,
        scratch_shapes=[pltpu.VMEM((tm, d), BF16)],
        compiler_params=_params("parallel", "arbitrary"),
        name="norm_matmul",
    )(x, nw.reshape(1, d), w_stack)
    return res if emit_xn else res[0]


def _shift_rows(x, prev8, n, first):
    rows = lax.broadcasted_iota(jnp.int32, x.shape, 0)
    out = pltpu.roll(x, n, 0)
    for j in range(n):
        fill = jnp.where(first, 0.0, prev8[8 - n + j:8 - n + j + 1, :])
        out = jnp.where(rows == j, fill, out)
    return out


def _rwkv_prep_kernel(p_ref, pl_ref, pp_ref, ppl_ref, mu_ref, mul_ref, w0_ref, a0_ref,
                      kk_ref, ka_ref, rk_ref, w2_ref, a2_ref, g2_ref,
                      r_o, k_o, v_o, kk_o, b_o, lw_o, g_o, bonus_o, *, tiles_per_seq):
    first = (pl.program_id(0) % tiles_per_seq) == 0
    w = r_o.shape[1]
    p = p_ref[...]
    ps = p + (_shift_rows(p, pp_ref[...], 1, first) - p) * mu_ref[...]
    q = pl_ref[...]
    qs = q + (_shift_rows(q, ppl_ref[...], 1, first) - q) * mul_ref[...]
    r = ps[:, :w]
    k = ps[:, w:2 * w]
    v = ps[:, 2 * w:]
    z = w0_ref[...] + _bdot(jnp.tanh(qs), w2_ref[...])
    w_log = jnp.minimum(z, 0.0) - jnp.log(1.0 + jnp.exp(-jnp.abs(z))) - 0.5
    a = jax.nn.sigmoid(a0_ref[...] + _bdot(qs, a2_ref[...]))
    g = _bdot(jax.nn.sigmoid(qs), g2_ref[...])
    kk = k * kk_ref[...]
    kk = kk * lax.rsqrt(jnp.maximum(_head_sum(kk * kk), 1e-24))
    k2 = k * (1.0 + (a - 1.0) * ka_ref[...])
    r_o[...] = r.astype(BF16)
    k_o[...] = k2.astype(BF16)
    v_o[...] = v.astype(BF16)
    kk_o[...] = kk.astype(BF16)
    b_o[...] = (kk * a).astype(BF16)
    lw_o[...] = -jnp.exp(w_log)
    g_o[...] = g.astype(BF16)
    bonus_o[...] = (_head_sum(r * k2 * rk_ref[...]) * v).astype(BF16)


def _rwkv_prep(proj, seq, w, mu_rkv, mu_lora, w0, a0, k_k, k_a, r_k, w2p, a2p, g2p, *, tt=128):
    m = proj.shape[0]
    lora_blk = (3 * w) // LORA_PAD
    row = lambda i: (i, 0)
    prev = lambda i: (jnp.maximum(i * (tt // 8) - 1, 0), 0)
    prev_l = lambda i: (jnp.maximum(i * (tt // 8) - 1, 0), lora_blk)
    const = lambda i: (0, 0)
    vec = pl.BlockSpec((1, w), const)
    out = jax.ShapeDtypeStruct((m, w), F32)
    out16 = jax.ShapeDtypeStruct((m, w), BF16)
    return pl.pallas_call(
        functools.partial(_rwkv_prep_kernel, tiles_per_seq=seq // tt),
        grid=(m // tt,),
        in_specs=[pl.BlockSpec((tt, 3 * w), row),
                  pl.BlockSpec((tt, LORA_PAD), lambda i: (i, lora_blk)),
                  pl.BlockSpec((8, 3 * w), prev),
                  pl.BlockSpec((8, LORA_PAD), prev_l),
                  pl.BlockSpec((1, 3 * w), const),
                  pl.BlockSpec((1, LORA_PAD), const),
                  vec, vec, vec, vec, vec,
                  pl.BlockSpec((LORA_PAD, w), const),
                  pl.BlockSpec((LORA_PAD, w), const),
                  pl.BlockSpec((LORA_PAD, w), const)],
        out_specs=[pl.BlockSpec((tt, w), row)] * 8,
        out_shape=[out16] * 5 + [out, out16, out16],
        compiler_params=_params("parallel"),
        name="rwkv_prep",
    )(proj, proj, proj, proj, mu_rkv, mu_lora, w0, a0, k_k, k_a, r_k, w2p, a2p, g2p)


def _wkv_chunk(r, k, v, kk, b, lw, s_prev):
    c = WKV_CHUNK
    pairs = range(len(r))
    units = [(p, h) for p in pairs for h in range(2)]
    row = lax.broadcasted_iota(jnp.int32, (c, c), 0)
    col = lax.broadcasted_iota(jnp.int32, (c, c), 1)
    lower = row >= col
    strict = row > col
    eye = (row == col).astype(F32)
    low16 = lower.astype(BF16)
    head_a = lax.broadcasted_iota(jnp.int32, (c, LANES), 1) < HEAD_DIM
    nt = lambda x, y: lax.dot_general(x, y, NT_DIMS, preferred_element_type=F32)
    mm = lambda x, y: jnp.dot(x, y, preferred_element_type=F32)

    cum = [_dot_exact_lhs(low16, lw[p]) for p in pairs]
    cend = [cum[p][c - 1:c, :] for p in pairs]
    cc = [cum[p] - cum[p][c // 2 - 1:c // 2, :] for p in pairs]
    e_neg = [jnp.exp(-cc[p]) for p in pairs]
    at = [-kk[p] * jnp.exp(cc[p] - lw[p]) for p in pairs]
    rt = [r[p] * jnp.exp(cc[p]) for p in pairs]
    lhs = [jnp.concatenate([jnp.where(head_a, at[p], 0.0), jnp.where(head_a, 0.0, at[p]),
                            jnp.where(head_a, rt[p], 0.0), jnp.where(head_a, 0.0, rt[p])],
                           axis=0).astype(BF16) for p in pairs]
    rhs = [jnp.concatenate([b[p] * e_neg[p], k[p] * e_neg[p]], axis=0).astype(BF16) for p in pairs]
    gram = [nt(lhs[p], rhs[p]) for p in pairs]
    s0 = [s_prev[p].astype(BF16) for p in pairs]
    v16 = [v[p].astype(BF16) for p in pairs]
    state_u = [nt((-kk[p] * jnp.exp(cum[p] - lw[p])).astype(BF16), s0[p]) for p in pairs]
    state_y = [nt((r[p] * jnp.exp(cum[p])).astype(BF16), s0[p]) for p in pairs]

    pw = [jnp.where(strict, gram[p][h * c:(h + 1) * c, :c], 0.0) for p, h in units]
    inv = [eye + x for x in pw]
    for _ in range(6):
        pw16 = [x.astype(BF16) for x in pw]
        pw = [mm(x, x) for x in pw16]
        inv = [i + _bdot(i, x) for i, x in zip(inv, pw)]
    a_ak = [jnp.where(strict, gram[p][h * c:(h + 1) * c, c:], 0.0).astype(BF16) for p, h in units]
    rhs_u = [state_u[p] + mm(a_ak[2 * p + h], v16[p]) for p, h in units]
    us = [_bdot(i, x) for i, x in zip(inv, rhs_u)]
    u16 = [jnp.where(head_a, us[2 * p], us[2 * p + 1]).astype(BF16) for p in pairs]

    a_rb = [jnp.where(lower, gram[p][(2 + h) * c:(3 + h) * c, :c], 0.0).astype(BF16)
            for p, h in units]
    a_rk = [jnp.where(lower, gram[p][(2 + h) * c:(3 + h) * c, c:], 0.0).astype(BF16)
            for p, h in units]
    yh = [mm(a_rb[2 * p + h], u16[p]) + mm(a_rk[2 * p + h], v16[p]) for p, h in units]
    ys = [state_y[p] + jnp.where(head_a, yh[2 * p], yh[2 * p + 1]) for p in pairs]

    e_end = [jnp.exp(cend[p] - cum[p]) for p in pairs]
    upd = [lax.dot_general(jnp.concatenate([u16[p], v16[p]], axis=0),
                           jnp.concatenate([b[p] * e_end[p], k[p] * e_end[p]], axis=0).astype(BF16),
                           TN_DIMS, preferred_element_type=F32) for p in pairs]
    srow = lax.broadcasted_iota(jnp.int32, (LANES, LANES), 0) // HEAD_DIM
    scol = lax.broadcasted_iota(jnp.int32, (LANES, LANES), 1) // HEAD_DIM
    same_head = srow == scol
    s_new = [s_prev[p] * jnp.exp(cend[p]) + jnp.where(same_head, upd[p], 0.0) for p in pairs]
    return ys, s_new


def _wkv_kernel(r_ref, k_ref, v_ref, kk_ref, b_ref, lw_ref, *refs):
    n_cast = (len(refs) - 2) // 2
    cast_in, y_ref, cast_out, s_ref = (refs[:n_cast], refs[n_cast], refs[n_cast + 1:-1], refs[-1])

    @pl.when(pl.program_id(2) == 0)
    def _():
        s_ref[...] = jnp.zeros_like(s_ref)

    for src, dst in zip(cast_in, cast_out):
        dst[...] = src[...].astype(BF16)

    n_pairs = s_ref.shape[0]
    lanes = [slice(p * LANES, (p + 1) * LANES) for p in range(n_pairs)]
    load = lambda ref: [ref[:, sl].astype(F32) for sl in lanes]
    ys, s_new = _wkv_chunk(load(r_ref), load(k_ref), load(v_ref), load(kk_ref), load(b_ref),
                           load(lw_ref), [s_ref[p] for p in range(n_pairs)])
    for p in range(n_pairs):
        y_ref[:, lanes[p]] = ys[p]
        s_ref[p] = s_new[p]


def _wkv_scan(r, k, v, kk, b, lw, batch, seq, cast_stacks, layer, *, pairs=8):
    m, w = r.shape
    c = WKV_CHUNK
    nchunk = seq // c
    groups = w // (pairs * LANES)
    n_steps = batch * groups * nchunk
    step = lambda bi, p, ci: (bi * groups + p) * nchunk + ci
    spec = pl.BlockSpec((c, pairs * LANES), lambda bi, p, ci: (bi * nchunk + ci, p))
    cast_in, cast_out, cast_shape = [], [], []
    for stack in cast_stacks:
        rows, cols = stack.shape[1:]
        slab = rows // n_steps
        assert slab * n_steps == rows and slab % 16 == 0
        cast_in.append(pl.BlockSpec((None, slab, cols),
                                    lambda bi, p, ci: (layer, step(bi, p, ci), 0)))
        cast_out.append(pl.BlockSpec((slab, cols), lambda bi, p, ci: (step(bi, p, ci), 0)))
        cast_shape.append(jax.ShapeDtypeStruct((rows, cols), BF16))
    return pl.pallas_call(
        _wkv_kernel,
        grid=(batch, groups, nchunk),
        in_specs=[spec] * 6 + cast_in,
        out_specs=[spec] + cast_out,
        out_shape=[jax.ShapeDtypeStruct((m, w), F32)] + cast_shape,
        scratch_shapes=[pltpu.VMEM((pairs, LANES, LANES), F32)],
        compiler_params=_params("parallel", "parallel", "arbitrary"),
        name="wkv_scan",
    )(r, k, v, kk, b, lw, *cast_stacks)


def _mix_post_kernel(y_ref, g_ref, bonus_ref, lnw_ref, lnb_ref, bg_ref, cg_ref, hv_ref,
                     cgp_ref, hvp_ref, cw_ref, o_ref, *, tiles_per_seq, lane_pad):
    first = (pl.program_id(0) % tiles_per_seq) == 0
    w = y_ref.shape[1]
    y = y_ref[...]
    mean = _head_sum(y) * (1.0 / HEAD_DIM)
    d = y - mean
    var = _head_sum(d * d) * (1.0 / HEAD_DIM)
    yn = d * lax.rsqrt(var + GN_EPS) * lnw_ref[...] + lnb_ref[...]
    o_ref[:, :w] = ((yn + bonus_ref[...].astype(F32)) * g_ref[...].astype(F32)).astype(BF16)
    z = cg_ref[...] * hv_ref[...]
    zp = cgp_ref[...] * hvp_ref[...]
    cw = cw_ref[...]
    zc = (cw[0:1, :] * _shift_rows(z, zp, 2, first) + cw[1:2, :] * _shift_rows(z, zp, 1, first)
          + cw[2:3, :] * z)
    o_ref[:, w:] = (bg_ref[...] * zc)[:, lane_pad:lane_pad + w].astype(BF16)


def _mix_post(y, g, bonus, ln_w, ln_b, proj, conv_w, conv_start, seq, *, tt=128):
    m, w = y.shape
    lane_pad = conv_start % LANES
    win = w + (LANES if lane_pad else 0)
    starts = [conv_start - lane_pad + j * w for j in range(3)]
    assert starts[2] + win <= proj.shape[1], "proj must be lane-padded so every window is in bounds"
    cw = jnp.pad(conv_w.T, ((0, 8 - CONV_K), (lane_pad, win - w - lane_pad)))
    row = lambda i: (i, 0)
    const = lambda i: (0, 0)
    cur = lambda s: pl.BlockSpec((pl.Element(tt), pl.Element(win)), lambda i: (i * tt, s))
    prev = lambda s: pl.BlockSpec((pl.Element(8), pl.Element(win)),
                                  lambda i: (pl.multiple_of(jnp.maximum(i * tt - 8, 0), 8), s))
    return pl.pallas_call(
        functools.partial(_mix_post_kernel, tiles_per_seq=seq // tt, lane_pad=lane_pad),
        grid=(m // tt,),
        in_specs=[pl.BlockSpec((tt, w), row)] * 3
        + [pl.BlockSpec((1, w), const)] * 2
        + [cur(s) for s in starts]
        + [prev(starts[1]), prev(starts[2]), pl.BlockSpec((8, win), const)],
        out_specs=pl.BlockSpec((tt, 2 * w), row),
        out_shape=jax.ShapeDtypeStruct((m, 2 * w), BF16),
        compiler_params=_params("parallel"),
        name="mix_post",
    )(y, g, bonus, ln_w, ln_b, proj, proj, proj, proj, proj, cw)


def _matmul_residual_kernel(a_ref, w_ref, res_ref, o_ref):
    o_ref[...] = res_ref[...] + jnp.dot(a_ref[...], w_ref[...].astype(BF16),
                                        preferred_element_type=F32)


def _matmul_residual(a_bf16, w_stack, layer, res, *, tm=1024, tn=512):
    m, kd = a_bf16.shape
    n = w_stack.shape[2]
    return pl.pallas_call(
        _matmul_residual_kernel,
        grid=(m // tm, n // tn),
        in_specs=[pl.BlockSpec((tm, kd), lambda i, j: (i, 0)),
                  pl.BlockSpec((None, kd, tn), lambda i, j: (layer, 0, j)),
                  pl.BlockSpec((tm, tn), lambda i, j: (i, j))],
        out_specs=pl.BlockSpec((tm, tn), lambda i, j: (i, j)),
        out_shape=jax.ShapeDtypeStruct((m, n), F32),
        compiler_params=_params("parallel", "arbitrary"),
        name="matmul_residual",
    )(a_bf16, w_stack, res)


def _top_values(s, count):
    tops = []
    for _ in range(count):
        mx = jnp.max(s, axis=0, keepdims=True)
        tops.append(mx)
        s = jnp.where(s >= mx, -jnp.inf, s)
    return tops


def _peer_select_kernel(q_ref, keys_ref, th_o, e1_o, s2_o, st_o):
    q = q_ref[...]
    half = q.shape[1] // 2
    sc = []
    for c in range(2):
        qc = q[:, c * half:(c + 1) * half]
        sc.append(lax.dot_general(keys_ref[0, c].astype(BF16), qc.astype(BF16), NT_DIMS,
                                  preferred_element_type=F32))
    top1 = _top_values(sc[0], PEER_TOPK)
    top2 = _top_values(sc[1], PEER_TOPK)
    t1 = jnp.concatenate(top1, axis=0)
    t2 = jnp.concatenate(top2, axis=0)
    row8 = lax.broadcasted_iota(jnp.int32, (8, t2.shape[1]), 0)
    groups = [top1[0] + t2]
    for m_ in range(1, 8):
        n_keep = PEER_TOPK // (m_ + 1)
        g8 = top1[m_] + t2[:8]
        groups.append(g8 if n_keep >= 8 else jnp.where(row8 < n_keep, g8, -jnp.inf))
    groups.append(t1[8:] + top2[0])
    cand = jnp.concatenate(groups, axis=0)
    best = _top_values(cand, PEER_TOPK)
    cmax = best[0]
    z = jnp.exp(best[0] - cmax)
    for bv in best[1:]:
        z = z + jnp.exp(bv - cmax)
    tau = best[-1]
    th = jnp.full_like(sc[0], jnp.inf)
    for m_ in range(PEER_TOPK):
        th_m = jnp.min(jnp.where(top1[m_] + t2 >= tau, t2, jnp.inf), axis=0, keepdims=True)
        th = jnp.where(sc[0] == top1[m_], th_m, th)
    th_o[0] = th
    e1_o[0] = jnp.exp(sc[0] - top1[0]) * (1.0 / z)
    s2_o[0] = sc[1]
    st_o[0] = jnp.concatenate([top2[0], jnp.zeros((STAT_ROWS - 1, z.shape[1]), F32)], axis=0)


def _peer_select(q, sub_keys, *, tt=512):
    assert PEER_TOPK == 16, "candidate groups in _peer_select_kernel are laid out for K = 16"
    m = q.shape[0]
    heads, _, nk, half = sub_keys.shape
    big = jax.ShapeDtypeStruct((heads, nk, m), F32)
    bspec = pl.BlockSpec((1, nk, tt), lambda i, h: (h, 0, i))
    return pl.pallas_call(
        _peer_select_kernel,
        grid=(m // tt, heads),
        in_specs=[pl.BlockSpec((tt, 2 * half), lambda i, h: (i, h)),
                  pl.BlockSpec((1, 2, nk, half), lambda i, h: (h, 0, 0, 0))],
        out_specs=[bspec] * 3 + [pl.BlockSpec((1, STAT_ROWS, tt), lambda i, h: (h, 0, i))],
        out_shape=[big] * 3 + [jax.ShapeDtypeStruct((heads, STAT_ROWS, m), F32)],
        compiler_params=_params("parallel", "arbitrary"),
        name="peer_select",
    )(q, sub_keys)


def _peer_mlp_kernel(x_ref, down_ref, up_ref, th_ref, e1_ref, s2_ref, st_ref, o_ref, e2_scr,
                     gate_scr):
    heads = s2_ref.shape[0]
    sub = 2 * N_KEYS

    @pl.when(pl.program_id(1) == 0)
    def _():
        o_ref[...] = jnp.zeros_like(o_ref)
        for h in range(heads):
            e2_scr[h] = jnp.exp(s2_ref[h] - st_ref[h, 0:1, :])

    tm = x_ref.shape[0]
    te = down_ref.shape[0]
    first_row = pl.program_id(1) * (te // N_KEYS)

    def gate_unit(ii, t0):
        tk = slice(t0, t0 + LANES)
        acc = None
        for h in range(heads):
            th = th_ref[h, pl.ds(first_row + ii, 1), :][:, tk]
            e1 = e1_ref[h, pl.ds(first_row + ii, 1), :][:, tk]
            term = jnp.where(s2_ref[h, :, tk] >= th, e1 * e2_scr[h, :, tk], 0.0)
            acc = term if acc is None else acc + term
        gate_scr[tk, ii * N_KEYS:(ii + 1) * N_KEYS] = acc.T

    for ii in range(te // N_KEYS):
        for t0 in range(0, tm, LANES):
            gate_unit(ii, t0)
    x = x_ref[...]
    hid = []
    for j in range(te // sub):
        cols = slice(j * sub, (j + 1) * sub)
        act = lax.dot_general(x, down_ref[cols, :], NT_DIMS, preferred_element_type=F32)
        gelu = 0.5 * act * (1.0 + lax.erf(act * (2.0 ** -0.5)))
        hid.append((gelu * gate_scr[:, cols]).astype(BF16))
    o_ref[...] += jnp.dot(jnp.concatenate(hid, axis=1), up_ref[...], preferred_element_type=F32)


def _peer_mlp(xn_bf16, down_bf16, up_bf16, th, e1, s2, stats, *, tm=512, te=512):
    m, d = xn_bf16.shape
    ne = down_bf16.shape[0]
    heads, nk, _ = s2.shape
    big = pl.BlockSpec((heads, nk, tm), lambda i, e: (0, 0, i))
    return pl.pallas_call(
        _peer_mlp_kernel,
        grid=(m // tm, ne // te),
        in_specs=[pl.BlockSpec((tm, d), lambda i, e: (i, 0)),
                  pl.BlockSpec((te, d), lambda i, e: (e, 0)),
                  pl.BlockSpec((te, d), lambda i, e: (e, 0)),
                  big, big, big,
                  pl.BlockSpec((heads, STAT_ROWS, tm), lambda i, e: (0, 0, i))],
        out_specs=pl.BlockSpec((tm, d), lambda i, e: (i, 0)),
        out_shape=jax.ShapeDtypeStruct((m, d), F32),
        scratch_shapes=[pltpu.VMEM((heads, nk, tm), F32), pltpu.VMEM((tm, te), F32)],
        compiler_params=pltpu.CompilerParams(dimension_semantics=("parallel", "arbitrary"),
                                             vmem_limit_bytes=PEER_VMEM_LIMIT),
        name="peer_mlp",
    )(xn_bf16, down_bf16, up_bf16, th, e1, s2, stats)


def _final_norm_kernel(h_ref, p_ref, nw_ref, o_ref):
    x = h_ref[...] + p_ref[...]
    y = x * lax.rsqrt(jnp.mean(x * x, axis=-1, keepdims=True) + NORM_EPS)
    o_ref[...] = y * nw_ref[...]


def _final_norm(h, p, nw, *, tt=256):
    m, d = h.shape
    row = lambda i: (i, 0)
    return pl.pallas_call(
        _final_norm_kernel,
        grid=(m // tt,),
        in_specs=[pl.BlockSpec((tt, d), row), pl.BlockSpec((tt, d), row),
                  pl.BlockSpec((1, d), lambda i: (0, 0))],
        out_specs=pl.BlockSpec((tt, d), row),
        out_shape=jax.ShapeDtypeStruct((m, d), F32),
        compiler_params=_params("parallel"),
        name="final_norm",
    )(h, p, nw.reshape(1, d))


def _pad_rows(mat, start, total):
    return jnp.pad(mat, ((start, total - start - mat.shape[0]), (0, 0)))


def kernel(x, norm1_w, w_in, mu_shift, w0, w2, a0, a2, g2, k_k, k_a, r_k, ln_x_w, ln_x_b,
           conv_w, w_out, norm2_w, w_q, sub_keys, expert_down, expert_up, norm_f_w):
    batch, seq, d = x.shape
    depth = w_in.shape[0]
    w = w0.shape[1]
    rwkv_in = 3 * w + DECAY_LORA + ICLR_LORA + GATE_LORA
    lora_w = rwkv_in - 3 * w
    m = batch * seq
    h = x.reshape(m, d)
    row = lambda t: t.reshape(1, -1)
    for l in range(depth):
        proj = _norm_matmul(h, norm1_w[l], jnp.swapaxes(w_in, 1, 2), l, emit_xn=False,
                            w_rows_are_outputs=True, tm=1024)
        mu = mu_shift[l]
        mu_lora = jnp.pad(mu[3 * w:], (0, LORA_PAD - lora_w))
        w2p = _pad_rows(w2[l], 0, LORA_PAD).astype(BF16)
        a2p = _pad_rows(a2[l], DECAY_LORA, LORA_PAD).astype(BF16)
        g2p = _pad_rows(g2[l], DECAY_LORA + ICLR_LORA, LORA_PAD).astype(BF16)
        r, k2, v, kk, bvec, lw, g, bonus = _rwkv_prep(
            proj, seq, w, row(mu[:3 * w]), row(mu_lora), row(w0[l]), row(a0[l]), row(k_k[l]),
            row(k_a[l]), row(r_k[l]), w2p, a2p, g2p)
        y, down16, up16, w_out16, w_q16 = _wkv_scan(
            r, k2, v, kk, bvec, lw, batch, seq, (expert_down, expert_up, w_out, w_q), l)
        mixed = _mix_post(y, g, bonus, row(ln_x_w[l]), row(ln_x_b[l]), proj, conv_w[l], rwkv_in,
                          seq)
        h = _matmul_residual(mixed, w_out16[None], 0, h)
        q, xn = _norm_matmul(h, norm2_w[l], w_q16[None], 0, emit_xn=True,
                             w_rows_are_outputs=False, tm=1024, tn=512)
        th, e1, s2, stats = _peer_select(q, sub_keys[l])
        peer = _peer_mlp(xn, down16, up16, th, e1, s2, stats)
        if l + 1 < depth:
            h = h + peer
    return _final_norm(h, peer, norm_f_w).reshape(batch, seq, d)
```

```python
import functools

import jax
import jax.numpy as jnp
from jax import lax
from jax.experimental import pallas as pl
from jax.experimental.pallas import tpu as pltpu

F32 = jnp.float32
BF16 = jnp.bfloat16

HEAD_DIM = 64
LANES = 128
NORM_EPS = 1e-6
GN_EPS = 64e-5
DECAY_LORA = 96
ICLR_LORA = 96
GATE_LORA = 256
LORA_PAD = 512
CONV_K = 3
PEER_HEADS = 8
N_KEYS = 128
PEER_TOPK = 16
WKV_CHUNK = 128
STAT_ROWS = 8
NORM_ROWS = 128
VMEM_LIMIT = 56 * 1024 * 1024
PEER_VMEM_LIMIT = 60 * 1024 * 1024

NT_DIMS = (((1,), (1,)), ((), ()))
TN_DIMS = (((0,), (0,)), ((), ()))


def _params(*sem):
    return pltpu.CompilerParams(dimension_semantics=sem, vmem_limit_bytes=VMEM_LIMIT)


def _bdot(a, b):
    return jnp.dot(a.astype(BF16), b.astype(BF16), preferred_element_type=F32)


def _split3(x):
    hi = x.astype(BF16)
    r1 = x - hi.astype(F32)
    mid = r1.astype(BF16)
    lo = (r1 - mid.astype(F32)).astype(BF16)
    return hi, mid, lo


def _dot_exact_lhs(a_bf16, x):
    hi, mid, lo = _split3(x)
    d = lambda t: jnp.dot(a_bf16, t, preferred_element_type=F32)
    return d(hi) + d(mid) + d(lo)


def _dot_exact_rhs(x, b_bf16):
    hi = x.astype(BF16)
    lo = (x - hi.astype(F32)).astype(BF16)
    d = lambda t: jnp.dot(t, b_bf16, preferred_element_type=F32)
    return d(hi) + d(lo)


def _head_ones():
    r = lax.broadcasted_iota(jnp.int32, (LANES, LANES), 0) // HEAD_DIM
    c = lax.broadcasted_iota(jnp.int32, (LANES, LANES), 1) // HEAD_DIM
    return (r == c).astype(BF16)


def _head_sum(x):
    ones = _head_ones()
    parts = [_dot_exact_rhs(x[:, j:j + LANES], ones) for j in range(0, x.shape[1], LANES)]
    return jnp.concatenate(parts, axis=1)


def _norm_matmul_kernel(x_ref, nw_ref, w_ref, o_ref, *rest, emit_xn, w_rows_are_outputs, n_valid):
    xn_scr = rest[-1]

    @pl.when(pl.program_id(1) == 0)
    def _():
        for r0 in range(0, x_ref.shape[0], NORM_ROWS):
            x = x_ref[r0:r0 + NORM_ROWS, :]
            y = x * lax.rsqrt(jnp.mean(x * x, axis=-1, keepdims=True) + NORM_EPS)
            xn_scr[r0:r0 + NORM_ROWS, :] = (y * nw_ref[...]).astype(BF16)

    w = w_ref[...].astype(BF16)
    if w_rows_are_outputs:
        out = lax.dot_general(xn_scr[...], w, NT_DIMS, preferred_element_type=F32)
    else:
        out = jnp.dot(xn_scr[...], w, preferred_element_type=F32)
    tn = o_ref.shape[1]
    if n_valid % tn:
        col = pl.program_id(1) * tn + lax.broadcasted_iota(jnp.int32, out.shape, 1)
        out = jnp.where(col < n_valid, out, 0.0)
    o_ref[...] = out
    if emit_xn:
        @pl.when(pl.program_id(1) == 0)
        def _():
            rest[0][...] = xn_scr[...]


def _norm_matmul(x, nw, w_stack, layer, *, emit_xn, w_rows_are_outputs, tm, tn=512):
    m, d = x.shape
    n = w_stack.shape[1] if w_rows_are_outputs else w_stack.shape[2]
    n_blocks = pl.cdiv(n, tn)
    if w_rows_are_outputs:
        w_spec = pl.BlockSpec((None, tn, d), lambda i, j: (layer, j, 0))
    else:
        w_spec = pl.BlockSpec((None, d, tn), lambda i, j: (layer, 0, j))
    out_shape = [jax.ShapeDtypeStruct((m, n_blocks * tn), F32)]
    out_specs = [pl.BlockSpec((tm, tn), lambda i, j: (i, j))]
    if emit_xn:
        out_shape.append(jax.ShapeDtypeStruct((m, d), BF16))
        out_specs.append(pl.BlockSpec((tm, d), lambda i, j: (i, 0)))
    res = pl.pallas_call(
        functools.partial(_norm_matmul_kernel, emit_xn=emit_xn,
                          w_rows_are_outputs=w_rows_are_outputs, n_valid=n),
        grid=(m // tm, n_blocks),
        in_specs=[pl.BlockSpec((tm, d), lambda i, j: (i, 0), pipeline_mode=pl.Buffered(1)),
                  pl.BlockSpec((1, d), lambda i, j: (0, 0)),
                  w_spec],
        out_specs=out_specs,
        out_shape=out_shape,
        scratch_shapes=[pltpu.VMEM((tm, d), BF16)],
        compiler_params=_params("parallel", "arbitrary"),
        name="norm_matmul",
    )(x, nw.reshape(1, d), w_stack)
    return res if emit_xn else res[0]


def _shift_rows(x, prev8, n, first):
    rows = lax.broadcasted_iota(jnp.int32, x.shape, 0)
    out = pltpu.roll(x, n, 0)
    for j in range(n):
        fill = jnp.where(first, 0.0, prev8[8 - n + j:8 - n + j + 1, :])
        out = jnp.where(rows == j, fill, out)
    return out


def _rwkv_prep_kernel(p_ref, pl_ref, pp_ref, ppl_ref, mu_ref, mul_ref, w0_ref, a0_ref,
                      kk_ref, ka_ref, rk_ref, w2_ref, a2_ref, g2_ref,
                      r_o, k_o, v_o, kk_o, b_o, lw_o, g_o, bonus_o, *, tiles_per_seq):
    first = (pl.program_id(0) % tiles_per_seq) == 0
    w = r_o.shape[1]
    p = p_ref[...]
    ps = p + (_shift_rows(p, pp_ref[...], 1, first) - p) * mu_ref[...]
    q = pl_ref[...]
    qs = q + (_shift_rows(q, ppl_ref[...], 1, first) - q) * mul_ref[...]
    r = ps[:, :w]
    k = ps[:, w:2 * w]
    v = ps[:, 2 * w:]
    z = w0_ref[...] + _bdot(jnp.tanh(qs), w2_ref[...])
    w_log = jnp.minimum(z, 0.0) - jnp.log(1.0 + jnp.exp(-jnp.abs(z))) - 0.5
    a = jax.nn.sigmoid(a0_ref[...] + _bdot(qs, a2_ref[...]))
    g = _bdot(jax.nn.sigmoid(qs), g2_ref[...])
    kk = k * kk_ref[...]
    kk = kk * lax.rsqrt(jnp.maximum(_head_sum(kk * kk), 1e-24))
    k2 = k * (1.0 + (a - 1.0) * ka_ref[...])
    r_o[...] = r.astype(BF16)
    k_o[...] = k2.astype(BF16)
    v_o[...] = v.astype(BF16)
    kk_o[...] = kk.astype(BF16)
    b_o[...] = (kk * a).astype(BF16)
    lw_o[...] = -jnp.exp(w_log)
    g_o[...] = g.astype(BF16)
    bonus_o[...] = (_head_sum(r * k2 * rk_ref[...]) * v).astype(BF16)


def _rwkv_prep(proj, seq, w, mu_rkv, mu_lora, w0, a0, k_k, k_a, r_k, w2p, a2p, g2p, *, tt=128):
    m = proj.shape[0]
    lora_blk = (3 * w) // LORA_PAD
    row = lambda i: (i, 0)
    prev = lambda i: (jnp.maximum(i * (tt // 8) - 1, 0), 0)
    prev_l = lambda i: (jnp.maximum(i * (tt // 8) - 1, 0), lora_blk)
    const = lambda i: (0, 0)
    vec = pl.BlockSpec((1, w), const)
    out = jax.ShapeDtypeStruct((m, w), F32)
    out16 = jax.ShapeDtypeStruct((m, w), BF16)
    return pl.pallas_call(
        functools.partial(_rwkv_prep_kernel, tiles_per_seq=seq // tt),
        grid=(m // tt,),
        in_specs=[pl.BlockSpec((tt, 3 * w), row),
                  pl.BlockSpec((tt, LORA_PAD), lambda i: (i, lora_blk)),
                  pl.BlockSpec((8, 3 * w), prev),
                  pl.BlockSpec((8, LORA_PAD), prev_l),
                  pl.BlockSpec((1, 3 * w), const),
                  pl.BlockSpec((1, LORA_PAD), const),
                  vec, vec, vec, vec, vec,
                  pl.BlockSpec((LORA_PAD, w), const),
                  pl.BlockSpec((LORA_PAD, w), const),
                  pl.BlockSpec((LORA_PAD, w), const)],
        out_specs=[pl.BlockSpec((tt, w), row)] * 8,
        out_shape=[out16] * 5 + [out, out16, out16],
        compiler_params=_params("parallel"),
        name="rwkv_prep",
    )(proj, proj, proj, proj, mu_rkv, mu_lora, w0, a0, k_k, k_a, r_k, w2p, a2p, g2p)


def _wkv_chunk(r, k, v, kk, b, lw, s_prev):
    c = WKV_CHUNK
    pairs = range(len(r))
    units = [(p, h) for p in pairs for h in range(2)]
    row = lax.broadcasted_iota(jnp.int32, (c, c), 0)
    col = lax.broadcasted_iota(jnp.int32, (c, c), 1)
    lower = row >= col
    strict = row > col
    eye = (row == col).astype(F32)
    low16 = lower.astype(BF16)
    head_a = lax.broadcasted_iota(jnp.int32, (c, LANES), 1) < HEAD_DIM
    nt = lambda x, y: lax.dot_general(x, y, NT_DIMS, preferred_element_type=F32)
    mm = lambda x, y: jnp.dot(x, y, preferred_element_type=F32)

    cum = [_dot_exact_lhs(low16, lw[p]) for p in pairs]
    cend = [cum[p][c - 1:c, :] for p in pairs]
    cc = [cum[p] - cum[p][c // 2 - 1:c // 2, :] for p in pairs]
    e_neg = [jnp.exp(-cc[p]) for p in pairs]
    at = [-kk[p] * jnp.exp(cc[p] - lw[p]) for p in pairs]
    rt = [r[p] * jnp.exp(cc[p]) for p in pairs]
    lhs = [jnp.concatenate([jnp.where(head_a, at[p], 0.0), jnp.where(head_a, 0.0, at[p]),
                            jnp.where(head_a, rt[p], 0.0), jnp.where(head_a, 0.0, rt[p])],
                           axis=0).astype(BF16) for p in pairs]
    rhs = [jnp.concatenate([b[p] * e_neg[p], k[p] * e_neg[p]], axis=0).astype(BF16) for p in pairs]
    gram = [nt(lhs[p], rhs[p]) for p in pairs]
    s0 = [s_prev[p].astype(BF16) for p in pairs]
    v16 = [v[p].astype(BF16) for p in pairs]
    state_u = [nt((-kk[p] * jnp.exp(cum[p] - lw[p])).astype(BF16), s0[p]) for p in pairs]
    state_y = [nt((r[p] * jnp.exp(cum[p])).astype(BF16), s0[p]) for p in pairs]

    pw = [jnp.where(strict, gram[p][h * c:(h + 1) * c, :c], 0.0) for p, h in units]
    inv = [eye + x for x in pw]
    for _ in range(6):
        pw16 = [x.astype(BF16) for x in pw]
        pw = [mm(x, x) for x in pw16]
        inv = [i + _bdot(i, x) for i, x in zip(inv, pw)]
    a_ak = [jnp.where(strict, gram[p][h * c:(h + 1) * c, c:], 0.0).astype(BF16) for p, h in units]
    rhs_u = [state_u[p] + mm(a_ak[2 * p + h], v16[p]) for p, h in units]
    us = [_bdot(i, x) for i, x in zip(inv, rhs_u)]
    u16 = [jnp.where(head_a, us[2 * p], us[2 * p + 1]).astype(BF16) for p in pairs]

    a_rb = [jnp.where(lower, gram[p][(2 + h) * c:(3 + h) * c, :c], 0.0).astype(BF16)
            for p, h in units]
    a_rk = [jnp.where(lower, gram[p][(2 + h) * c:(3 + h) * c, c:], 0.0).astype(BF16)
            for p, h in units]
    yh = [mm(a_rb[2 * p + h], u16[p]) + mm(a_rk[2 * p + h], v16[p]) for p, h in units]
    ys = [state_y[p] + jnp.where(head_a, yh[2 * p], yh[2 * p + 1]) for p in pairs]

    e_end = [jnp.exp(cend[p] - cum[p]) for p in pairs]
    upd = [lax.dot_general(jnp.concatenate([u16[p], v16[p]], axis=0),
                           jnp.concatenate([b[p] * e_end[p], k[p] * e_end[p]], axis=0).astype(BF16),
                           TN_DIMS, preferred_element_type=F32) for p in pairs]
    srow = lax.broadcasted_iota(jnp.int32, (LANES, LANES), 0) // HEAD_DIM
    scol = lax.broadcasted_iota(jnp.int32, (LANES, LANES), 1) // HEAD_DIM
    same_head = srow == scol
    s_new = [s_prev[p] * jnp.exp(cend[p]) + jnp.where(same_head, upd[p], 0.0) for p in pairs]
    return ys, s_new


def _wkv_kernel(r_ref, k_ref, v_ref, kk_ref, b_ref, lw_ref, *refs):
    n_cast = (len(refs) - 2) // 2
    cast_in, y_ref, cast_out, s_ref = (refs[:n_cast], refs[n_cast], refs[n_cast + 1:-1], refs[-1])

    @pl.when(pl.program_id(2) == 0)
    def _():
        s_ref[...] = jnp.zeros_like(s_ref)

    for src, dst in zip(cast_in, cast_out):
        dst[...] = src[...].astype(BF16)

    n_pairs = s_ref.shape[0]
    lanes = [slice(p * LANES, (p + 1) * LANES) for p in range(n_pairs)]
    load = lambda ref: [ref[:, sl].astype(F32) for sl in lanes]
    ys, s_new = _wkv_chunk(load(r_ref), load(k_ref), load(v_ref), load(kk_ref), load(b_ref),
                           load(lw_ref), [s_ref[p] for p in range(n_pairs)])
    for p in range(n_pairs):
        y_ref[:, lanes[p]] = ys[p]
        s_ref[p] = s_new[p]


def _wkv_scan(r, k, v, kk, b, lw, batch, seq, cast_stacks, layer, *, pairs=8):
    m, w = r.shape
    c = WKV_CHUNK
    nchunk = seq // c
    groups = w // (pairs * LANES)
    n_steps = batch * groups * nchunk
    step = lambda bi, p, ci: (bi * groups + p) * nchunk + ci
    spec = pl.BlockSpec((c, pairs * LANES), lambda bi, p, ci: (bi * nchunk + ci, p))
    cast_in, cast_out, cast_shape = [], [], []
    for stack in cast_stacks:
        rows, cols = stack.shape[1:]
        slab = rows // n_steps
        assert slab * n_steps == rows and slab % 16 == 0
        cast_in.append(pl.BlockSpec((None, slab, cols),
                                    lambda bi, p, ci: (layer, step(bi, p, ci), 0)))
        cast_out.append(pl.BlockSpec((slab, cols), lambda bi, p, ci: (step(bi, p, ci), 0)))
        cast_shape.append(jax.ShapeDtypeStruct((rows, cols), BF16))
    return pl.pallas_call(
        _wkv_kernel,
        grid=(batch, groups, nchunk),
        in_specs=[spec] * 6 + cast_in,
        out_specs=[spec] + cast_out,
        out_shape=[jax.ShapeDtypeStruct((m, w), F32)] + cast_shape,
        scratch_shapes=[pltpu.VMEM((pairs, LANES, LANES), F32)],
        compiler_params=_params("parallel", "parallel", "arbitrary"),
        name="wkv_scan",
    )(r, k, v, kk, b, lw, *cast_stacks)


def _mix_post_kernel(y_ref, g_ref, bonus_ref, lnw_ref, lnb_ref, bg_ref, cg_ref, hv_ref,
                     cgp_ref, hvp_ref, cw_ref, o_ref, *, tiles_per_seq, lane_pad):
    first = (pl.program_id(0) % tiles_per_seq) == 0
    w = y_ref.shape[1]
    y = y_ref[...]
    mean = _head_sum(y) * (1.0 / HEAD_DIM)
    d = y - mean
    var = _head_sum(d * d) * (1.0 / HEAD_DIM)
    yn = d * lax.rsqrt(var + GN_EPS) * lnw_ref[...] + lnb_ref[...]
    o_ref[:, :w] = ((yn + bonus_ref[...].astype(F32)) * g_ref[...].astype(F32)).astype(BF16)
    z = cg_ref[...] * hv_ref[...]
    zp = cgp_ref[...] * hvp_ref[...]
    cw = cw_ref[...]
    zc = (cw[0:1, :] * _shift_rows(z, zp, 2, first) + cw[1:2, :] * _shift_rows(z, zp, 1, first)
          + cw[2:3, :] * z)
    o_ref[:, w:] = (bg_ref[...] * zc)[:, lane_pad:lane_pad + w].astype(BF16)


def _mix_post(y, g, bonus, ln_w, ln_b, proj, conv_w, conv_start, seq, *, tt=128):
    m, w = y.shape
    lane_pad = conv_start % LANES
    win = w + (LANES if lane_pad else 0)
    starts = [conv_start - lane_pad + j * w for j in range(3)]
    assert starts[2] + win <= proj.shape[1], "proj must be lane-padded so every window is in bounds"
    cw = jnp.pad(conv_w.T, ((0, 8 - CONV_K), (lane_pad, win - w - lane_pad)))
    row = lambda i: (i, 0)
    const = lambda i: (0, 0)
    cur = lambda s: pl.BlockSpec((pl.Element(tt), pl.Element(win)), lambda i: (i * tt, s))
    prev = lambda s: pl.BlockSpec((pl.Element(8), pl.Element(win)),
                                  lambda i: (pl.multiple_of(jnp.maximum(i * tt - 8, 0), 8), s))
    return pl.pallas_call(
        functools.partial(_mix_post_kernel, tiles_per_seq=seq // tt, lane_pad=lane_pad),
        grid=(m // tt,),
        in_specs=[pl.BlockSpec((tt, w), row)] * 3
        + [pl.BlockSpec((1, w), const)] * 2
        + [cur(s) for s in starts]
        + [prev(starts[1]), prev(starts[2]), pl.BlockSpec((8, win), const)],
        out_specs=pl.BlockSpec((tt, 2 * w), row),
        out_shape=jax.ShapeDtypeStruct((m, 2 * w), BF16),
        compiler_params=_params("parallel"),
        name="mix_post",
    )(y, g, bonus, ln_w, ln_b, proj, proj, proj, proj, proj, cw)


def _matmul_residual_kernel(a_ref, w_ref, res_ref, o_ref):
    o_ref[...] = res_ref[...] + jnp.dot(a_ref[...], w_ref[...].astype(BF16),
                                        preferred_element_type=F32)


def _matmul_residual(a_bf16, w_stack, layer, res, *, tm=1024, tn=512):
    m, kd = a_bf16.shape
    n = w_stack.shape[2]
    return pl.pallas_call(
        _matmul_residual_kernel,
        grid=(m // tm, n // tn),
        in_specs=[pl.BlockSpec((tm, kd), lambda i, j: (i, 0)),
                  pl.BlockSpec((None, kd, tn), lambda i, j: (layer, 0, j)),
                  pl.BlockSpec((tm, tn), lambda i, j: (i, j))],
        out_specs=pl.BlockSpec((tm, tn), lambda i, j: (i, j)),
        out_shape=jax.ShapeDtypeStruct((m, n), F32),
        compiler_params=_params("parallel", "arbitrary"),
        name="matmul_residual",
    )(a_bf16, w_stack, res)


def _top_values(s, count):
    tops = []
    for _ in range(count):
        mx = jnp.max(s, axis=0, keepdims=True)
        tops.append(mx)
        s = jnp.where(s >= mx, -jnp.inf, s)
    return tops


def _peer_select_kernel(q_ref, keys_ref, th_o, e1_o, s2_o, st_o):
    q = q_ref[...]
    half = q.shape[1] // 2
    sc = []
    for c in range(2):
        qc = q[:, c * half:(c + 1) * half]
        sc.append(lax.dot_general(keys_ref[0, c].astype(BF16), qc.astype(BF16), NT_DIMS,
                                  preferred_element_type=F32))
    top1 = _top_values(sc[0], PEER_TOPK)
    top2 = _top_values(sc[1], PEER_TOPK)
    t1 = jnp.concatenate(top1, axis=0)
    t2 = jnp.concatenate(top2, axis=0)
    row8 = lax.broadcasted_iota(jnp.int32, (8, t2.shape[1]), 0)
    groups = [top1[0] + t2]
    for m_ in range(1, 8):
        n_keep = PEER_TOPK // (m_ + 1)
        g8 = top1[m_] + t2[:8]
        groups.append(g8 if n_keep >= 8 else jnp.where(row8 < n_keep, g8, -jnp.inf))
    groups.append(t1[8:] + top2[0])
    cand = jnp.concatenate(groups, axis=0)
    best = _top_values(cand, PEER_TOPK)
    cmax = best[0]
    z = jnp.exp(best[0] - cmax)
    for bv in best[1:]:
        z = z + jnp.exp(bv - cmax)
    tau = best[-1]
    th = jnp.full_like(sc[0], jnp.inf)
    for m_ in range(PEER_TOPK):
        th_m = jnp.min(jnp.where(top1[m_] + t2 >= tau, t2, jnp.inf), axis=0, keepdims=True)
        th = jnp.where(sc[0] == top1[m_], th_m, th)
    th_o[0] = th
    e1_o[0] = jnp.exp(sc[0] - top1[0]) * (1.0 / z)
    s2_o[0] = sc[1]
    st_o[0] = jnp.concatenate([top2[0], jnp.zeros((STAT_ROWS - 1, z.shape[1]), F32)], axis=0)


def _peer_select(q, sub_keys, *, tt=512):
    assert PEER_TOPK == 16, "candidate groups in _peer_select_kernel are laid out for K = 16"
    m = q.shape[0]
    heads, _, nk, half = sub_keys.shape
    big = jax.ShapeDtypeStruct((heads, nk, m), F32)
    bspec = pl.BlockSpec((1, nk, tt), lambda i, h: (h, 0, i))
    return pl.pallas_call(
        _peer_select_kernel,
        grid=(m // tt, heads),
        in_specs=[pl.BlockSpec((tt, 2 * half), lambda i, h: (i, h)),
                  pl.BlockSpec((1, 2, nk, half), lambda i, h: (h, 0, 0, 0))],
        out_specs=[bspec] * 3 + [pl.BlockSpec((1, STAT_ROWS, tt), lambda i, h: (h, 0, i))],
        out_shape=[big] * 3 + [jax.ShapeDtypeStruct((heads, STAT_ROWS, m), F32)],
        compiler_params=_params("parallel", "arbitrary"),
        name="peer_select",
    )(q, sub_keys)


def _peer_mlp_kernel(x_ref, down_ref, up_ref, th_ref, e1_ref, s2_ref, st_ref, o_ref, e2_scr,
                     gate_scr):
    heads = s2_ref.shape[0]
    sub = 2 * N_KEYS

    @pl.when(pl.program_id(1) == 0)
    def _():
        o_ref[...] = jnp.zeros_like(o_ref)
        for h in range(heads):
            e2_scr[h] = jnp.exp(s2_ref[h] - st_ref[h, 0:1, :])

    tm = x_ref.shape[0]
    te = down_ref.shape[0]
    first_row = pl.program_id(1) * (te // N_KEYS)

    def gate_unit(ii, t0):
        tk = slice(t0, t0 + LANES)
        acc = None
        for h in range(heads):
            th = th_ref[h, pl.ds(first_row + ii, 1), :][:, tk]
            e1 = e1_ref[h, pl.ds(first_row + ii, 1), :][:, tk]
            term = jnp.where(s2_ref[h, :, tk] >= th, e1 * e2_scr[h, :, tk], 0.0)
            acc = term if acc is None else acc + term
        gate_scr[tk, ii * N_KEYS:(ii + 1) * N_KEYS] = acc.T

    for ii in range(te // N_KEYS):
        for t0 in range(0, tm, LANES):
            gate_unit(ii, t0)
    x = x_ref[...]
    hid = []
    for j in range(te // sub):
        cols = slice(j * sub, (j + 1) * sub)
        act = lax.dot_general(x, down_ref[cols, :], NT_DIMS, preferred_element_type=F32)
        gelu = 0.5 * act * (1.0 + lax.erf(act * (2.0 ** -0.5)))
        hid.append((gelu * gate_scr[:, cols]).astype(BF16))
    o_ref[...] += jnp.dot(jnp.concatenate(hid, axis=1), up_ref[...], preferred_element_type=F32)


def _peer_mlp(xn_bf16, down_bf16, up_bf16, th, e1, s2, stats, *, tm=512, te=512):
    m, d = xn_bf16.shape
    ne = down_bf16.shape[0]
    heads, nk, _ = s2.shape
    big = pl.BlockSpec((heads, nk, tm), lambda i, e: (0, 0, i))
    return pl.pallas_call(
        _peer_mlp_kernel,
        grid=(m // tm, ne // te),
        in_specs=[pl.BlockSpec((tm, d), lambda i, e: (i, 0)),
                  pl.BlockSpec((te, d), lambda i, e: (e, 0)),
                  pl.BlockSpec((te, d), lambda i, e: (e, 0)),
                  big, big, big,
                  pl.BlockSpec((heads, STAT_ROWS, tm), lambda i, e: (0, 0, i))],
        out_specs=pl.BlockSpec((tm, d), lambda i, e: (i, 0)),
        out_shape=jax.ShapeDtypeStruct((m, d), F32),
        scratch_shapes=[pltpu.VMEM((heads, nk, tm), F32), pltpu.VMEM((tm, te), F32)],
        compiler_params=pltpu.CompilerParams(dimension_semantics=("parallel", "arbitrary"),
                                             vmem_limit_bytes=PEER_VMEM_LIMIT),
        name="peer_mlp",
    )(xn_bf16, down_bf16, up_bf16, th, e1, s2, stats)


def _final_norm_kernel(h_ref, p_ref, nw_ref, o_ref):
    x = h_ref[...] + p_ref[...]
    y = x * lax.rsqrt(jnp.mean(x * x, axis=-1, keepdims=True) + NORM_EPS)
    o_ref[...] = y * nw_ref[...]


def _final_norm(h, p, nw, *, tt=256):
    m, d = h.shape
    row = lambda i: (i, 0)
    return pl.pallas_call(
        _final_norm_kernel,
        grid=(m // tt,),
        in_specs=[pl.BlockSpec((tt, d), row), pl.BlockSpec((tt, d), row),
                  pl.BlockSpec((1, d), lambda i: (0, 0))],
        out_specs=pl.BlockSpec((tt, d), row),
        out_shape=jax.ShapeDtypeStruct((m, d), F32),
        compiler_params=_params("parallel"),
        name="final_norm",
    )(h, p, nw.reshape(1, d))


def _pad_rows(mat, start, total):
    return jnp.pad(mat, ((start, total - start - mat.shape[0]), (0, 0)))


def kernel(x, norm1_w, w_in, mu_shift, w0, w2, a0, a2, g2, k_k, k_a, r_k, ln_x_w, ln_x_b,
           conv_w, w_out, norm2_w, w_q, sub_keys, expert_down, expert_up, norm_f_w):
    batch, seq, d = x.shape
    depth = w_in.shape[0]
    w = w0.shape[1]
    rwkv_in = 3 * w + DECAY_LORA + ICLR_LORA + GATE_LORA
    lora_w = rwkv_in - 3 * w
    m = batch * seq
    h = x.reshape(m, d)
    row = lambda t: t.reshape(1, -1)
    for l in range(depth):
        proj = _norm_matmul(h, norm1_w[l], jnp.swapaxes(w_in, 1, 2), l, emit_xn=False,
                            w_rows_are_outputs=True, tm=1024)
        mu = mu_shift[l]
        mu_lora = jnp.pad(mu[3 * w:], (0, LORA_PAD - lora_w))
        w2p = _pad_rows(w2[l], 0, LORA_PAD).astype(BF16)
        a2p = _pad_rows(a2[l], DECAY_LORA, LORA_PAD).astype(BF16)
        g2p = _pad_rows(g2[l], DECAY_LORA + ICLR_LORA, LORA_PAD).astype(BF16)
        r, k2, v, kk, bvec, lw, g, bonus = _rwkv_prep(
            proj, seq, w, row(mu[:3 * w]), row(mu_lora), row(w0[l]), row(a0[l]), row(k_k[l]),
            row(k_a[l]), row(r_k[l]), w2p, a2p, g2p)
        y, down16, up16, w_out16, w_q16 = _wkv_scan(
            r, k2, v, kk, bvec, lw, batch, seq, (expert_down, expert_up, w_out, w_q), l)
        mixed = _mix_post(y, g, bonus, row(ln_x_w[l]), row(ln_x_b[l]), proj, conv_w[l], rwkv_in,
                          seq)
        h = _matmul_residual(mixed, w_out16[None], 0, h)
        q, xn = _norm_matmul(h, norm2_w[l], w_q16[None], 0, emit_xn=True,
                             w_rows_are_outputs=False, tm=1024, tn=512)
        th, e1, s2, stats = _peer_select(q, sub_keys[l])
        peer = _peer_mlp(xn, down16, up16, th, e1, s2, stats)
        if l + 1 < depth:
            h = h + peer
    return _final_norm(h, peer, norm_f_w).reshape(batch, seq, d)
```
